```python
import jax, jax.numpy as jnp
from jax import lax
import numpy as np

D_MODEL = 1024
BATCH = 16
SEQ = 4096
DEPTH = 4

N_HEADS = 8
QK_NOPE_DIM = 64
QK_ROPE_DIM = 32
QK_DIM = QK_NOPE_DIM + QK_ROPE_DIM
V_HEAD_DIM = 64
Q_LORA_RANK = 384
KV_LORA_RANK = 256
ROPE_THETA = 10000.0
Q_BLOCK = 128
CONV_CHANNELS = 512
CONV_WIDTH = 31
FFN_HIDDEN = -(-8 * D_MODEL // (3 * 256)) * 256
N_MOD = 6
EPS = 1e-6
NEG_INF = -1e30

IN_WIDTHS = (Q_LORA_RANK, KV_LORA_RANK, QK_ROPE_DIM, 2 * CONV_CHANNELS, D_MODEL, D_MODEL)
SPLIT_IDX = (Q_LORA_RANK,
             Q_LORA_RANK + KV_LORA_RANK,
             Q_LORA_RANK + KV_LORA_RANK + QK_ROPE_DIM,
             Q_LORA_RANK + KV_LORA_RANK + QK_ROPE_DIM + 2 * CONV_CHANNELS,
             Q_LORA_RANK + KV_LORA_RANK + QK_ROPE_DIM + 2 * CONV_CHANNELS + D_MODEL)
IN_WIDTH = sum(IN_WIDTHS)

kernel_name = "hybrid_mla_conformer_adaln_block"


def rms_norm(x, gain):
    xf = x.astype(jnp.float32)
    y = xf * lax.rsqrt(jnp.mean(xf * xf, axis=-1, keepdims=True) + EPS)
    return (y * gain.astype(jnp.float32)).astype(x.dtype)


def layer_norm(x, gain, bias):
    xf = x.astype(jnp.float32)
    mu = jnp.mean(xf, axis=-1, keepdims=True)
    xc = xf - mu
    y = xc * lax.rsqrt(jnp.mean(xc * xc, axis=-1, keepdims=True) + EPS)
    return (y * gain.astype(jnp.float32) + bias.astype(jnp.float32)).astype(x.dtype)


def rope_tables(positions, dtype):
    inv_freq = ROPE_THETA ** (-jnp.arange(0, QK_ROPE_DIM, 2, dtype=jnp.float32) / QK_ROPE_DIM)
    ang = positions.astype(jnp.float32)[..., None] * inv_freq
    return jnp.cos(ang).astype(dtype), jnp.sin(ang).astype(dtype)


def apply_rope(t, cos, sin):
    t1, t2 = jnp.split(t, 2, axis=-1)
    return jnp.concatenate([t1 * cos - t2 * sin, t2 * cos + t1 * sin], axis=-1)


def mla_attention(q_nope, q_rope, k_nope, k_rope, v):
    seq = q_nope.shape[1]
    scale = QK_DIM ** -0.5
    outs = []
    for i in range(seq // Q_BLOCK):
        s0, s1 = i * Q_BLOCK, (i + 1) * Q_BLOCK
        logits = (jnp.einsum('bqhd,bkhd->bhqk', q_nope[:, s0:s1], k_nope[:, :s1])
                  + jnp.einsum('bqhr,bkr->bhqk', q_rope[:, s0:s1], k_rope[:, :s1]))
        logits = logits.astype(jnp.float32) * scale
        causal = (s0 + jnp.arange(Q_BLOCK))[:, None] >= jnp.arange(s1)[None, :]
        probs = jax.nn.softmax(jnp.where(causal, logits, NEG_INF), axis=-1).astype(v.dtype)
        outs.append(jnp.einsum('bhqk,bkhd->bqhd', probs, v[:, :s1]))
    return jnp.concatenate(outs, axis=1)


def causal_depthwise_conv(u, w, b):
    rhs = w.astype(u.dtype)[:, None, :]
    y = lax.conv_general_dilated(u, rhs, window_strides=(1,), padding=[(CONV_WIDTH - 1, 0)],
                                 dimension_numbers=('NWC', 'WIO', 'NWC'),
                                 feature_group_count=u.shape[-1])
    return y + b


def hybrid_layer(x, c_act, cos, sin, w_ada, b_ada, g_mix, w_in, g_q, w_uq, g_kv, w_ukv, w_o_attn,
                 w_dw, b_dw, g_cn, b_cn, w_pw2, w_out, g_ffn, w_gu, w_down):
    bsz, seq, _ = x.shape
    mod = (c_act @ w_ada + b_ada)[:, None, :]
    sh1, sc1, gt1, sh2, sc2, gt2 = jnp.split(mod, N_MOD, axis=-1)

    h = rms_norm(x, g_mix) * (1 + sc1) + sh1
    proj = h @ w_in
    q_lat, kv_lat, k_rope, glu_in, gate_a, gate_b = jnp.split(proj, SPLIT_IDX, axis=-1)

    q = (rms_norm(q_lat, g_q) @ w_uq).reshape(bsz, seq, N_HEADS, QK_DIM)
    q_nope = q[..., :QK_NOPE_DIM]
    q_rope = apply_rope(q[..., QK_NOPE_DIM:], cos[:, :, None, :], sin[:, :, None, :])
    k_rope = apply_rope(k_rope, cos, sin)
    kv = (rms_norm(kv_lat, g_kv) @ w_ukv).reshape(bsz, seq, N_HEADS, QK_NOPE_DIM + V_HEAD_DIM)
    k_nope, v = kv[..., :QK_NOPE_DIM], kv[..., QK_NOPE_DIM:]
    attn = mla_attention(q_nope, q_rope, k_nope, k_rope, v).reshape(bsz, seq, N_HEADS * V_HEAD_DIM)
    y_attn = attn @ w_o_attn

    glu_a, glu_b = jnp.split(glu_in, 2, axis=-1)
    u = causal_depthwise_conv(glu_a * jax.nn.sigmoid(glu_b), w_dw, b_dw)
    y_conv = jax.nn.silu(layer_norm(u, g_cn, b_cn)) @ w_pw2

    y = jax.nn.sigmoid(gate_a) * y_attn + jax.nn.sigmoid(gate_b) * y_conv
    x = x + gt1 * (y @ w_out)

    h = rms_norm(x, g_ffn) * (1 + sc2) + sh2
    g, up = jnp.split(h @ w_gu, 2, axis=-1)
    x = x + gt2 * ((jax.nn.silu(g) * up) @ w_down)
    return x


def setup_inputs(seed: int = 0) -> dict:
    key = jax.random.key(seed)
    ks = jax.random.split(key, 24)
    f32 = jnp.float32
    L, D = DEPTH, D_MODEL

    def w(k, shape, fan_in, mult=1.0):
        return jax.random.normal(k, shape, f32) * (mult * fan_in ** -0.5)

    def gain(k, shape):
        return 1.0 + 0.02 * jax.random.normal(k, shape, f32)

    def bias(k, shape):
        return 0.01 * jax.random.normal(k, shape, f32)

    x = jax.random.normal(ks[0], (BATCH, SEQ, D), f32)
    c = jax.random.normal(ks[1], (BATCH, D), f32)
    offset = jax.random.randint(ks[2], (BATCH, 1), 0, 1024, dtype=jnp.int32)
    positions = offset + jnp.arange(SEQ, dtype=jnp.int32)[None, :]
    return {
        "x": x,
        "c": c,
        "positions": positions,
        "w_ada": w(ks[3], (L, D, N_MOD * D), D, 0.5),
        "b_ada": bias(ks[4], (L, N_MOD * D)),
        "g_mix": gain(ks[5], (L, D)),
        "w_in": w(ks[6], (L, D, IN_WIDTH), D),
        "g_q": gain(ks[7], (L, Q_LORA_RANK)),
        "w_uq": w(ks[8], (L, Q_LORA_RANK, N_HEADS * QK_DIM), Q_LORA_RANK),
        "g_kv": gain(ks[9], (L, KV_LORA_RANK)),
        "w_ukv": w(ks[10], (L, KV_LORA_RANK, N_HEADS * (QK_NOPE_DIM + V_HEAD_DIM)), KV_LORA_RANK),
        "w_o_attn": w(ks[11], (L, N_HEADS * V_HEAD_DIM, D), N_HEADS * V_HEAD_DIM),
        "w_dw": w(ks[12], (L, CONV_WIDTH, CONV_CHANNELS), CONV_WIDTH),
        "b_dw": bias(ks[13], (L, CONV_CHANNELS)),
        "g_cn": gain(ks[14], (L, CONV_CHANNELS)),
        "b_cn": bias(ks[15], (L, CONV_CHANNELS)),
        "w_pw2": w(ks[16], (L, CONV_CHANNELS, D), CONV_CHANNELS),
        "w_out": w(ks[17], (L, D, D), D),
        "g_ffn": gain(ks[18], (L, D)),
        "w_gu": w(ks[19], (L, D, 2 * FFN_HIDDEN), D),
        "w_down": w(ks[20], (L, FFN_HIDDEN, D), FFN_HIDDEN),
        "g_final": gain(ks[21], (D,)),
    }


def reference(x, c, positions, w_ada, b_ada, g_mix, w_in, g_q, w_uq, g_kv, w_ukv, w_o_attn,
              w_dw, b_dw, g_cn, b_cn, w_pw2, w_out, g_ffn, w_gu, w_down, g_final):
    cos, sin = rope_tables(positions, x.dtype)
    c_act = jax.nn.silu(c)
    for l in range(DEPTH):
        x = hybrid_layer(x, c_act, cos, sin, w_ada[l], b_ada[l], g_mix[l], w_in[l], g_q[l], w_uq[l],
                         g_kv[l], w_ukv[l], w_o_attn[l], w_dw[l], b_dw[l], g_cn[l], b_cn[l],
                         w_pw2[l], w_out[l], g_ffn[l], w_gu[l], w_down[l])
    return rms_norm(x, g_final)
```

```python
import functools
import math

import jax
import jax.numpy as jnp
from jax import lax
from jax.experimental import pallas as pl
from jax.experimental.pallas import tpu as pltpu

N_HEADS = 8
QK_NOPE_DIM = 64
QK_ROPE_DIM = 32
QK_DIM = QK_NOPE_DIM + QK_ROPE_DIM
V_HEAD_DIM = 64
ROPE_THETA = 10000.0
CONV_WIDTH = 31
N_MOD = 6
EPS = 1e-6
NEG_INF = -1e30

LANES = 128
HEAD_PAD = LANES
HALF_ROPE = QK_ROPE_DIM // 2
CONV_HALO = 32
CONV_ROW_BLOCK = 64
VMEM_LIMIT_BYTES = 56 * 1024 * 1024

BF16 = jnp.bfloat16
F32 = jnp.float32


def _dot(a, b):
    return jnp.dot(a, b, preferred_element_type=F32)


def _rms(x, gain):
    return x * lax.rsqrt(jnp.mean(x * x, axis=-1, keepdims=True) + EPS) * gain


def _const_spec(shape, index_map):
    return pl.BlockSpec(shape, index_map, pipeline_mode=pl.Buffered(1))


def _rope_table_kernel(pos_ref, invf_ref, sign_ref, ct_ref, st_ref):
    ang = pos_ref[...].astype(F32) * invf_ref[...]
    ct_ref[...] = jnp.cos(ang)
    st_ref[...] = jnp.sin(ang) * sign_ref[...]


def _rope_tables(positions):
    n_tok = positions.size
    tt = min(n_tok, 2048)
    inv_freq = ROPE_THETA ** (-jnp.arange(0, QK_ROPE_DIM, 2, dtype=F32) / QK_ROPE_DIM)
    zeros = lambda n: jnp.zeros((n,), F32)
    invf = jnp.concatenate([zeros(QK_NOPE_DIM), inv_freq, inv_freq, zeros(HEAD_PAD - QK_DIM)])[None, :]
    sign = jnp.concatenate([zeros(QK_NOPE_DIM), -jnp.ones((HALF_ROPE,), F32), jnp.ones((HALF_ROPE,), F32),
                            zeros(HEAD_PAD - QK_DIM)])[None, :]
    pos_b = jnp.broadcast_to(positions.reshape(n_tok, 1), (n_tok, LANES))
    row = pl.BlockSpec((1, LANES), lambda i: (0, 0))
    tile = pl.BlockSpec((tt, LANES), lambda i: (i, 0))
    return pl.pallas_call(
        _rope_table_kernel,
        grid=(n_tok // tt,),
        in_specs=[tile, row, row],
        out_specs=[tile, tile],
        out_shape=[jax.ShapeDtypeStruct((n_tok, LANES), F32)] * 2,
        name="rope_tables",
    )(pos_b, invf, sign)


def _mod_kernel(c_ref, w_ref, b_ref, o_ref):
    c = c_ref[...]
    c_act = (c * jax.nn.sigmoid(c)).astype(BF16)
    o_ref[0] = _dot(c_act, w_ref[0].astype(BF16)) + b_ref[0]


def _modulation(c, w_ada, b_ada):
    depth, d_model, n_out = w_ada.shape
    bsz = c.shape[0]
    tn = d_model
    return pl.pallas_call(
        _mod_kernel,
        grid=(depth, n_out // tn),
        in_specs=[pl.BlockSpec((bsz, d_model), lambda l, j: (0, 0)),
                  pl.BlockSpec((1, d_model, tn), lambda l, j: (l, 0, j)),
                  pl.BlockSpec((1, 1, tn), lambda l, j: (l, 0, j))],
        out_specs=pl.BlockSpec((1, bsz, tn), lambda l, j: (l, 0, j)),
        out_shape=jax.ShapeDtypeStruct((depth, bsz, n_out), F32),
        name="adaln_modulation",
    )(c, w_ada, b_ada.reshape(depth, 1, n_out))


def _pre_attn_kernel(x_ref, mod_ref, gmix_ref, ct_ref, st_ref, win_ref, gq_ref, wq_ref, gkv_ref, wkv_ref,
                     wdw_ref, bdw_ref, gcn_ref, bcn_ref, wpw2_ref,
                     q_out, k_out, v_out, ga_out, gby_out, uext_ref, conv_ref, *, tm, dims):
    q_rank, kv_rank, conv_ch, d_model = dims
    s_idx = pl.program_id(1)
    x = x_ref[0]
    sh1 = mod_ref[0, 0, 0]
    sc1 = mod_ref[0, 0, 1]
    hb = (_rms(x, gmix_ref[0]) * (1.0 + sc1) + sh1).astype(BF16)

    c0 = 0
    def cols(width):
        nonlocal c0
        w = win_ref[0, :, c0:c0 + width]
        c0 += width
        return w

    ct = ct_ref[...]
    st = st_ref[...]
    q_scale = QK_DIM ** -0.5
    ctq = ct * q_scale
    stq = st * q_scale
    n_qk = N_HEADS * HEAD_PAD

    qn = _rms(_dot(hb, cols(q_rank)), gq_ref[0]).astype(BF16)
    q2 = _dot(qn, wq_ref[0])
    for h in range(N_HEADS):
        lo, hi = h * HEAD_PAD, (h + 1) * HEAD_PAD
        q_out[0, :, lo:hi] = (q2[:, lo:hi] * ctq + q2[:, n_qk + lo:n_qk + hi] * stq).astype(BF16)

    kvn = _rms(_dot(hb, cols(kv_rank)), gkv_ref[0]).astype(BF16)
    kr = _dot(hb, cols(HEAD_PAD)) * ct + _dot(hb, cols(HEAD_PAD)) * st
    kv2 = _dot(kvn, wkv_ref[0])
    for h in range(N_HEADS):
        lo, hi = h * HEAD_PAD, (h + 1) * HEAD_PAD
        k_out[0, :, lo:hi] = (kv2[:, lo:hi] + kr).astype(BF16)
    v_out[0] = kv2[:, n_qk:].astype(BF16)

    glu_a = _dot(hb, cols(conv_ch))
    glu_b = _dot(hb, cols(conv_ch))

    @pl.when(s_idx == 0)
    def _():
        uext_ref[0:CONV_HALO, :] = jnp.zeros((CONV_HALO, conv_ch), F32)

    uext_ref[CONV_HALO:CONV_HALO + tm, :] = glu_a * jax.nn.sigmoid(glu_b)
    first = CONV_HALO - (CONV_WIDTH - 1)
    for c in range(conv_ch // LANES):
        cl = slice(c * LANES, (c + 1) * LANES)
        for rb in range(tm // CONV_ROW_BLOCK):
            r0 = rb * CONV_ROW_BLOCK
            acc = jnp.broadcast_to(bdw_ref[0, :, cl], (CONV_ROW_BLOCK, LANES))
            for k in range(CONV_WIDTH):
                acc = acc + wdw_ref[0, k:k + 1, cl] * uext_ref[r0 + first + k:r0 + first + k + CONV_ROW_BLOCK, cl]
            conv_ref[r0:r0 + CONV_ROW_BLOCK, cl] = acc
    uext_ref[0:CONV_HALO, :] = uext_ref[tm:tm + CONV_HALO, :]

    u = conv_ref[...]
    mu = jnp.mean(u, axis=-1, keepdims=True)
    uc = u - mu
    un = uc * lax.rsqrt(jnp.mean(uc * uc, axis=-1, keepdims=True) + EPS) * gcn_ref[0] + bcn_ref[0]
    y_conv = _dot((un * jax.nn.sigmoid(un)).astype(BF16), wpw2_ref[0])

    ga_out[0] = jax.nn.sigmoid(_dot(hb, cols(d_model))).astype(BF16)
    gby_out[0] = (jax.nn.sigmoid(_dot(hb, cols(d_model))) * y_conv).astype(BF16)


def _pre_attention(layer, x, mod, ct, st, p, *, tm):
    bsz, seq, d_model = x.shape
    q_rank, kv_rank, conv_ch = p["g_q"].shape[-1], p["g_kv"].shape[-1], p["g_cn"].shape[-1]
    n_qk = N_HEADS * HEAD_PAD
    n_v = N_HEADS * V_HEAD_DIM
    n_s = seq // tm
    l = layer
    tok = lambda w: pl.BlockSpec((1, tm, w), lambda b, s: (b, s, 0))
    wspec = lambda a: _const_spec((1,) + a.shape[1:], lambda b, s: (l,) + (0,) * (a.ndim - 1))
    tab = pl.BlockSpec((tm, LANES), lambda b, s: (b * n_s + s, 0))
    weights = [p["w_in"], p["g_q"], p["w_q"], p["g_kv"], p["w_kv"], p["w_dw"], p["b_dw"], p["g_cn"], p["b_cn"], p["w_pw2"]]
    kern = functools.partial(_pre_attn_kernel, tm=tm, dims=(q_rank, kv_rank, conv_ch, d_model))
    out_bf16 = lambda w: jax.ShapeDtypeStruct((bsz, seq, w), BF16)
    return pl.pallas_call(
        kern,
        grid=(bsz, n_s),
        in_specs=[tok(d_model),
                  pl.BlockSpec((1, 1, N_MOD, 1, d_model), lambda b, s: (l, b, 0, 0, 0)),
                  wspec(p["g_mix"]), tab, tab] + [wspec(w) for w in weights],
        out_specs=[tok(n_qk), tok(n_qk), tok(n_v), tok(d_model), tok(d_model)],
        out_shape=[out_bf16(n_qk), out_bf16(n_qk), out_bf16(n_v), out_bf16(d_model), out_bf16(d_model)],
        scratch_shapes=[pltpu.VMEM((CONV_HALO + tm, conv_ch), F32), pltpu.VMEM((tm, conv_ch), F32)],
        compiler_params=pltpu.CompilerParams(dimension_semantics=("arbitrary", "arbitrary"),
                                             vmem_limit_bytes=VMEM_LIMIT_BYTES),
        name="pre_attention",
    )(x, mod, p["g_mix"], ct, st, *weights)


def _attn_kernel(q_ref, k_ref, v_ref, o_ref, m_ref, l_ref, acc_ref, *, tq):
    qi = pl.program_id(2)
    heads = q_ref.shape[-1] // HEAD_PAD
    m_ref[...] = jnp.full(m_ref.shape, NEG_INF, F32)
    l_ref[...] = jnp.zeros(l_ref.shape, F32)
    acc_ref[...] = jnp.zeros(acc_ref.shape, F32)
    n_rep = tq // LANES

    def step(j, masked):
        rows = pl.ds(pl.multiple_of(j * tq, tq), tq)
        v = v_ref[0, rows, :]
        for h in range(heads):
            hl = slice(h * HEAD_PAD, (h + 1) * HEAD_PAD)
            s = lax.dot_general(q_ref[0, :, hl], k_ref[0, rows, hl], (((1,), (1,)), ((), ())),
                                preferred_element_type=F32)
            if masked:
                row_id = lax.broadcasted_iota(jnp.int32, s.shape, 0)
                col_id = lax.broadcasted_iota(jnp.int32, s.shape, 1)
                s = jnp.where(row_id >= col_id, s, NEG_INF)
            m_prev = m_ref[h]
            m_new = jnp.maximum(m_prev, jnp.max(s, axis=1, keepdims=True))
            alpha = jnp.exp(m_prev - m_new)
            p = jnp.exp(s - jnp.concatenate([m_new] * n_rep, axis=1))
            l_ref[h] = alpha * l_ref[h] + jnp.sum(p, axis=1, keepdims=True)
            acc_ref[h] = alpha * acc_ref[h] + _dot(p.astype(BF16), v)
            m_ref[h] = m_new

    def body(j, carry):
        step(j, False)
        return carry

    lax.fori_loop(0, qi, body, 0)
    step(qi, True)

    lane = lax.broadcasted_iota(jnp.int32, (tq, LANES), 1)
    out = acc_ref[0] / l_ref[0]
    for h in range(1, heads):
        out = jnp.where(lane >= h * V_HEAD_DIM, acc_ref[h] / l_ref[h], out)
    o_ref[0] = out.astype(BF16)


def _attention(q, k, v, *, tq):
    bsz, seq, _ = q.shape
    heads_per_step = LANES // V_HEAD_DIM
    n_hp = N_HEADS // heads_per_step
    qk_w = heads_per_step * HEAD_PAD
    kern = functools.partial(_attn_kernel, tq=tq)
    return pl.pallas_call(
        kern,
        grid=(bsz, n_hp, seq // tq),
        in_specs=[pl.BlockSpec((1, tq, qk_w), lambda b, h, i: (b, i, h)),
                  pl.BlockSpec((1, seq, qk_w), lambda b, h, i: (b, 0, h)),
                  pl.BlockSpec((1, seq, LANES), lambda b, h, i: (b, 0, h))],
        out_specs=pl.BlockSpec((1, tq, LANES), lambda b, h, i: (b, i, h)),
        out_shape=jax.ShapeDtypeStruct((bsz, seq, N_HEADS * V_HEAD_DIM), BF16),
        scratch_shapes=[pltpu.VMEM((heads_per_step, tq, LANES), F32)] * 3,
        compiler_params=pltpu.CompilerParams(dimension_semantics=("arbitrary", "arbitrary", "arbitrary"),
                                             vmem_limit_bytes=VMEM_LIMIT_BYTES),
        name="mla_attention",
    )(q, k, v)


def _post_attn_kernel(x_ref, attn_ref, ga_ref, gby_ref, mod_ref, wo_ref, wout_ref, gffn_ref, wgu_ref, wdown_ref,
                      gfin_ref, o_ref, *, ffn_chunk, final):
    x = x_ref[0]
    gt1 = mod_ref[0, 0, 2]
    sh2 = mod_ref[0, 0, 3]
    sc2 = mod_ref[0, 0, 4]
    gt2 = mod_ref[0, 0, 5]
    y = ga_ref[0].astype(F32) * _dot(attn_ref[0], wo_ref[0]) + gby_ref[0].astype(F32)
    x1 = x + gt1 * _dot(y.astype(BF16), wout_ref[0])

    hb = (_rms(x1, gffn_ref[0]) * (1.0 + sc2) + sh2).astype(BF16)
    hidden = wdown_ref.shape[1]
    ffn = None
    for c0 in range(0, hidden, ffn_chunk):
        c1 = min(c0 + ffn_chunk, hidden)
        g = _dot(hb, wgu_ref[0, :, c0:c1])
        up = _dot(hb, wgu_ref[0, :, hidden + c0:hidden + c1])
        part = _dot((g * jax.nn.sigmoid(g) * up).astype(BF16), wdown_ref[0, c0:c1, :])
        ffn = part if ffn is None else ffn + part
    x2 = x1 + gt2 * ffn
    if final:
        x2 = _rms(x2, gfin_ref[...])
    o_ref[0] = x2


def _post_attention(layer, x, attn, ga, gby, mod, p, g_final, *, tm, final):
    bsz, seq, d_model = x.shape
    l = layer
    tok = lambda w: pl.BlockSpec((1, tm, w), lambda b, s: (b, s, 0))
    wspec = lambda a: _const_spec((1,) + a.shape[1:], lambda b, s: (l,) + (0,) * (a.ndim - 1))
    weights = [p["w_o"], p["w_out"], p["g_ffn"], p["w_gu"], p["w_down"]]
    kern = functools.partial(_post_attn_kernel, ffn_chunk=512, final=final)
    return pl.pallas_call(
        kern,
        grid=(bsz, seq // tm),
        in_specs=[tok(d_model), tok(attn.shape[-1]), tok(d_model), tok(d_model),
                  pl.BlockSpec((1, 1, N_MOD, 1, d_model), lambda b, s: (l, b, 0, 0, 0))]
                 + [wspec(w) for w in weights]
                 + [_const_spec((1, d_model), lambda b, s: (0, 0))],
        out_specs=tok(d_model),
        out_shape=jax.ShapeDtypeStruct((bsz, seq, d_model), F32),
        compiler_params=pltpu.CompilerParams(dimension_semantics=("arbitrary", "arbitrary"),
                                             vmem_limit_bytes=VMEM_LIMIT_BYTES),
        name="post_attention",
    )(x, attn, ga, gby, mod, *weights, g_final)


def _layout_params(w_in, g_mix, g_q, w_uq, g_kv, w_ukv, w_o_attn, w_dw, b_dw, g_cn, b_cn, w_pw2, w_out, g_ffn,
                   w_gu, w_down):
    depth, d_model, _ = w_in.shape
    q_rank, kv_rank, conv_ch = g_q.shape[-1], g_kv.shape[-1], g_cn.shape[-1]
    zeros = lambda *s: jnp.zeros((depth,) + s, F32)
    o = 0
    def take(width):
        nonlocal o
        piece = w_in[:, :, o:o + width]
        o += width
        return piece
    q_lat, kv_lat, k_rope = take(q_rank), take(kv_rank), take(QK_ROPE_DIM)
    glu, gate_a, gate_b = take(2 * conv_ch), take(d_model), take(d_model)
    kr1, kr2 = k_rope[..., :HALF_ROPE], k_rope[..., HALF_ROPE:]
    pad_lo, pad_hi = zeros(d_model, QK_NOPE_DIM), zeros(d_model, HEAD_PAD - QK_DIM)
    kr_pad = jnp.concatenate([pad_lo, kr1, kr2, pad_hi], -1)
    kr_swap = jnp.concatenate([pad_lo, kr2, kr1, pad_hi], -1)
    w_in_l = jnp.concatenate([q_lat, kv_lat, kr_pad, kr_swap, glu, gate_a, gate_b], -1).astype(BF16)

    uq = w_uq.reshape(depth, q_rank, N_HEADS, QK_DIM)
    nope, r1, r2 = uq[..., :QK_NOPE_DIM], uq[..., QK_NOPE_DIM:QK_NOPE_DIM + HALF_ROPE], uq[..., QK_NOPE_DIM + HALF_ROPE:]
    z_lo = zeros(q_rank, N_HEADS, QK_NOPE_DIM)
    z_hi = zeros(q_rank, N_HEADS, HEAD_PAD - QK_DIM)
    q_cat = jnp.concatenate([nope, r1, r2, z_hi], -1).reshape(depth, q_rank, N_HEADS * HEAD_PAD)
    q_swap = jnp.concatenate([z_lo, r2, r1, z_hi], -1).reshape(depth, q_rank, N_HEADS * HEAD_PAD)
    w_q = jnp.concatenate([q_cat, q_swap], -1).astype(BF16)

    ukv = w_ukv.reshape(depth, kv_rank, N_HEADS, QK_NOPE_DIM + V_HEAD_DIM)
    k_cat = jnp.concatenate([ukv[..., :QK_NOPE_DIM], zeros(kv_rank, N_HEADS, HEAD_PAD - QK_NOPE_DIM)], -1)
    w_kv = jnp.concatenate([k_cat.reshape(depth, kv_rank, N_HEADS * HEAD_PAD),
                            ukv[..., QK_NOPE_DIM:].reshape(depth, kv_rank, N_HEADS * V_HEAD_DIM)], -1).astype(BF16)
    row = lambda a: a.reshape(depth, 1, a.shape[-1])
    return dict(w_in=w_in_l, g_mix=row(g_mix), g_q=row(g_q), w_q=w_q, g_kv=row(g_kv), w_kv=w_kv,
                w_o=w_o_attn.astype(BF16), w_dw=w_dw, b_dw=row(b_dw), g_cn=row(g_cn), b_cn=row(b_cn),
                w_pw2=w_pw2.astype(BF16), w_out=w_out.astype(BF16), g_ffn=row(g_ffn),
                w_gu=w_gu.astype(BF16), w_down=w_down.astype(BF16))


def kernel(x, c, positions, w_ada, b_ada, g_mix, w_in, g_q, w_uq, g_kv, w_ukv, w_o_attn, w_dw, b_dw, g_cn, b_cn,
           w_pw2, w_out, g_ffn, w_gu, w_down, g_final):
    bsz, seq, d_model = x.shape
    depth = w_in.shape[0]
    tm = min(seq, 512)
    tq = min(seq, 512)
    p = _layout_params(w_in, g_mix, g_q, w_uq, g_kv, w_ukv, w_o_attn, w_dw, b_dw, g_cn, b_cn, w_pw2, w_out, g_ffn,
                       w_gu, w_down)
    ct, st = _rope_tables(positions)
    mod = _modulation(c, w_ada, b_ada).reshape(depth, bsz, N_MOD, 1, d_model)
    g_fin = g_final.reshape(1, d_model)
    for layer in range(depth):
        q, k, v, ga, gby = _pre_attention(layer, x, mod, ct, st, p, tm=tm)
        attn = _attention(q, k, v, tq=tq)
        x = _post_attention(layer, x, attn, ga, gby, mod, p, g_fin, tm=tm, final=layer == depth - 1)
    return x
```

```python
import functools
import math

import jax
import jax.numpy as jnp
from jax import lax
from jax.experimental import pallas as pl
from jax.experimental.pallas import tpu as pltpu

N_HEADS = 8
QK_NOPE_DIM = 64
QK_ROPE_DIM = 32
QK_DIM = QK_NOPE_DIM + QK_ROPE_DIM
V_HEAD_DIM = 64
ROPE_THETA = 10000.0
CONV_WIDTH = 31
N_MOD = 6
EPS = 1e-6
NEG_INF = -1e30

LANES = 128
HEAD_PAD = LANES
HALF_ROPE = QK_ROPE_DIM // 2
ONES_ROWS = 16
CONV_HALO = 32
CONV_ROW_BLOCK = 64
VMEM_LIMIT_BYTES = 56 * 1024 * 1024

BF16 = jnp.bfloat16
F32 = jnp.float32


def _dot(a, b):
    return jnp.dot(a, b, preferred_element_type=F32)


def _rms(x, gain):
    return x * lax.rsqrt(jnp.mean(x * x, axis=-1, keepdims=True) + EPS) * gain


def _const_spec(shape, index_map):
    return pl.BlockSpec(shape, index_map, pipeline_mode=pl.Buffered(1))


def _rope_table_kernel(pos_ref, invf_ref, sign_ref, ct_ref, st_ref):
    ang = pos_ref[...].astype(F32) * invf_ref[...]
    ct_ref[...] = jnp.cos(ang)
    st_ref[...] = jnp.sin(ang) * sign_ref[...]


def _rope_tables(positions):
    n_tok = positions.size
    tt = min(n_tok, 2048)
    inv_freq = ROPE_THETA ** (-jnp.arange(0, QK_ROPE_DIM, 2, dtype=F32) / QK_ROPE_DIM)
    zeros = lambda n: jnp.zeros((n,), F32)
    invf = jnp.concatenate([zeros(QK_NOPE_DIM), inv_freq, inv_freq, zeros(HEAD_PAD - QK_DIM)])[None, :]
    sign = jnp.concatenate([zeros(QK_NOPE_DIM), -jnp.ones((HALF_ROPE,), F32), jnp.ones((HALF_ROPE,), F32),
                            zeros(HEAD_PAD - QK_DIM)])[None, :]
    pos_b = jnp.broadcast_to(positions.reshape(n_tok, 1), (n_tok, LANES))
    row = pl.BlockSpec((1, LANES), lambda i: (0, 0))
    tile = pl.BlockSpec((tt, LANES), lambda i: (i, 0))
    return pl.pallas_call(
        _rope_table_kernel,
        grid=(n_tok // tt,),
        in_specs=[tile, row, row],
        out_specs=[tile, tile],
        out_shape=[jax.ShapeDtypeStruct((n_tok, LANES), F32)] * 2,
        name="rope_tables",
    )(pos_b, invf, sign)


def _mod_kernel(c_ref, w_ref, b_ref, o_ref):
    c = c_ref[...]
    c_act = (c * jax.nn.sigmoid(c)).astype(BF16)
    o_ref[0] = _dot(c_act, w_ref[0].astype(BF16)) + b_ref[0]


def _modulation(c, w_ada, b_ada):
    depth, d_model, n_out = w_ada.shape
    bsz = c.shape[0]
    tn = d_model
    return pl.pallas_call(
        _mod_kernel,
        grid=(depth, n_out // tn),
        in_specs=[pl.BlockSpec((bsz, d_model), lambda l, j: (0, 0)),
                  pl.BlockSpec((1, d_model, tn), lambda l, j: (l, 0, j)),
                  pl.BlockSpec((1, 1, tn), lambda l, j: (l, 0, j))],
        out_specs=pl.BlockSpec((1, bsz, tn), lambda l, j: (l, 0, j)),
        out_shape=jax.ShapeDtypeStruct((depth, bsz, n_out), F32),
        name="adaln_modulation",
    )(c, w_ada, b_ada.reshape(depth, 1, n_out))


def _pre_attn_kernel(x_ref, mod_ref, gmix_ref, ct_ref, st_ref, win_ref, gq_ref, wq_ref, gkv_ref, wk_ref, wvt_ref,
                     wdw_ref, bdw_ref, gcn_ref, bcn_ref, wpw2_ref,
                     q_out, k_out, vt_out, ga_out, gby_out, uext_ref, conv_ref, *, tm, dims):
    q_rank, kv_rank, conv_ch, d_model = dims
    s_idx = pl.program_id(1)
    x = x_ref[0]
    sh1 = mod_ref[0, 0, 0]
    sc1 = mod_ref[0, 0, 1]
    hb = (_rms(x, gmix_ref[0]) * (1.0 + sc1) + sh1).astype(BF16)

    c0 = 0
    def cols(width):
        nonlocal c0
        w = win_ref[0, :, c0:c0 + width]
        c0 += width
        return w

    ct = ct_ref[...]
    st = st_ref[...]
    q_scale = QK_DIM ** -0.5 * math.log2(math.e)
    ctq = ct * q_scale
    stq = st * q_scale
    n_qk = N_HEADS * HEAD_PAD

    qn = _rms(_dot(hb, cols(q_rank)), gq_ref[0]).astype(BF16)
    q2 = _dot(qn, wq_ref[0])
    for h in range(N_HEADS):
        lo, hi = h * HEAD_PAD, (h + 1) * HEAD_PAD
        q_out[0, :, lo:hi] = (q2[:, lo:hi] * ctq + q2[:, n_qk + lo:n_qk + hi] * stq).astype(BF16)

    kvn = _rms(_dot(hb, cols(kv_rank)), gkv_ref[0]).astype(BF16)
    kr = _dot(hb, cols(HEAD_PAD)) * ct + _dot(hb, cols(HEAD_PAD)) * st
    k2 = _dot(kvn, wk_ref[0])
    for h in range(N_HEADS):
        lo, hi = h * HEAD_PAD, (h + 1) * HEAD_PAD
        k_out[0, :, lo:hi] = (k2[:, lo:hi] + kr).astype(BF16)
    vt_out[0] = lax.dot_general(wvt_ref[0], kvn, (((1,), (1,)), ((), ())), preferred_element_type=F32).astype(BF16)

    glu_a = _dot(hb, cols(conv_ch))
    glu_b = _dot(hb, cols(conv_ch))

    @pl.when(s_idx == 0)
    def _():
        uext_ref[0:CONV_HALO, :] = jnp.zeros((CONV_HALO, conv_ch), F32)

    uext_ref[CONV_HALO:CONV_HALO + tm, :] = glu_a * jax.nn.sigmoid(glu_b)
    first = CONV_HALO - (CONV_WIDTH - 1)
    for c in range(conv_ch // LANES):
        cl = slice(c * LANES, (c + 1) * LANES)
        for rb in range(tm // CONV_ROW_BLOCK):
            r0 = rb * CONV_ROW_BLOCK
            acc = jnp.broadcast_to(bdw_ref[0, :, cl], (CONV_ROW_BLOCK, LANES))
            for k in range(CONV_WIDTH):
                acc = acc + wdw_ref[0, k:k + 1, cl] * uext_ref[r0 + first + k:r0 + first + k + CONV_ROW_BLOCK, cl]
            conv_ref[r0:r0 + CONV_ROW_BLOCK, cl] = acc
    uext_ref[0:CONV_HALO, :] = uext_ref[tm:tm + CONV_HALO, :]

    u = conv_ref[...]
    mu = jnp.mean(u, axis=-1, keepdims=True)
    uc = u - mu
    un = uc * lax.rsqrt(jnp.mean(uc * uc, axis=-1, keepdims=True) + EPS) * gcn_ref[0] + bcn_ref[0]
    y_conv = _dot((un * jax.nn.sigmoid(un)).astype(BF16), wpw2_ref[0])

    ga_out[0] = jax.nn.sigmoid(_dot(hb, cols(d_model))).astype(BF16)
    gby_out[0] = (jax.nn.sigmoid(_dot(hb, cols(d_model))) * y_conv).astype(BF16)


def _pre_attention(layer, x, mod, ct, st, p, *, tm):
    bsz, seq, d_model = x.shape
    q_rank, kv_rank, conv_ch = p["g_q"].shape[-1], p["g_kv"].shape[-1], p["g_cn"].shape[-1]
    n_qk = N_HEADS * HEAD_PAD
    n_v = N_HEADS * V_HEAD_DIM
    n_s = seq // tm
    l = layer
    tok = lambda w: pl.BlockSpec((1, tm, w), lambda b, s: (b, s, 0))
    wspec = lambda a: _const_spec((1,) + a.shape[1:], lambda b, s: (l,) + (0,) * (a.ndim - 1))
    tab = pl.BlockSpec((tm, LANES), lambda b, s: (b * n_s + s, 0))
    weights = [p["w_in"], p["g_q"], p["w_q"], p["g_kv"], p["w_k"], p["w_vt"], p["w_dw"], p["b_dw"], p["g_cn"],
               p["b_cn"], p["w_pw2"]]
    kern = functools.partial(_pre_attn_kernel, tm=tm, dims=(q_rank, kv_rank, conv_ch, d_model))
    out_bf16 = lambda w: jax.ShapeDtypeStruct((bsz, seq, w), BF16)
    return pl.pallas_call(
        kern,
        grid=(bsz, n_s),
        in_specs=[tok(d_model),
                  pl.BlockSpec((1, 1, N_MOD, 1, d_model), lambda b, s: (l, b, 0, 0, 0)),
                  wspec(p["g_mix"]), tab, tab] + [wspec(w) for w in weights],
        out_specs=[tok(n_qk), tok(n_qk), pl.BlockSpec((1, n_v, tm), lambda b, s: (b, 0, s)), tok(d_model),
                   tok(d_model)],
        out_shape=[out_bf16(n_qk), out_bf16(n_qk), jax.ShapeDtypeStruct((bsz, n_v, seq), BF16), out_bf16(d_model),
                   out_bf16(d_model)],
        scratch_shapes=[pltpu.VMEM((CONV_HALO + tm, conv_ch), F32), pltpu.VMEM((tm, conv_ch), F32)],
        compiler_params=pltpu.CompilerParams(dimension_semantics=("arbitrary", "arbitrary"),
                                             vmem_limit_bytes=VMEM_LIMIT_BYTES),
        name="pre_attention",
    )(x, mod, p["g_mix"], ct, st, *weights)


def _attn_kernel(q_ref, k_ref, vt_ref, o_ref, m_ref, acc_ref, sa_ref, sb_ref, *, tq):
    qi = pl.program_id(2)
    heads = q_ref.shape[-1] // HEAD_PAD
    m_ref[...] = jnp.full(m_ref.shape, NEG_INF, F32)
    acc_ref[...] = jnp.zeros(acc_ref.shape, F32)
    ones_rows = jnp.ones((ONES_ROWS, tq), BF16)

    def key_rows(j):
        return pl.ds(pl.multiple_of(j * tq, tq), tq)

    def scores(j, s_ref):
        for h in range(heads):
            hl = slice(h * HEAD_PAD, (h + 1) * HEAD_PAD)
            s_ref[h] = lax.dot_general(k_ref[0, key_rows(j), hl], q_ref[0, :, hl], (((1,), (1,)), ((), ())),
                                       preferred_element_type=F32)

    def update(j, s_ref, masked):
        for h in range(heads):
            vl = slice(h * V_HEAD_DIM, (h + 1) * V_HEAD_DIM)
            s_t = s_ref[h]
            if masked:
                key_id = lax.broadcasted_iota(jnp.int32, s_t.shape, 0)
                qry_id = lax.broadcasted_iota(jnp.int32, s_t.shape, 1)
                s_t = jnp.where(key_id <= qry_id, s_t, NEG_INF)
            m_prev = m_ref[h]
            m_new = jnp.maximum(m_prev, jnp.max(s_t, axis=0, keepdims=True))
            alpha = jnp.exp2(m_prev - m_new)
            p_t = jnp.exp2(s_t - m_new).astype(BF16)
            v_ext = jnp.concatenate([vt_ref[0, vl, key_rows(j)], ones_rows], axis=0)
            acc_ref[h] = alpha * acc_ref[h] + _dot(v_ext, p_t)
            m_ref[h] = m_new

    scores(0, sa_ref)

    def pair(t, carry):
        j = 2 * t
        scores(j + 1, sb_ref)
        update(j, sa_ref, False)
        scores(j + 2, sa_ref)
        update(j + 1, sb_ref, False)
        return carry

    lax.fori_loop(0, qi // 2, pair, 0)
    odd = qi % 2 == 1

    @pl.when(odd)
    def _():
        scores(qi, sb_ref)
        update(qi - 1, sa_ref, False)
        update(qi, sb_ref, True)

    @pl.when(jnp.logical_not(odd))
    def _():
        update(qi, sa_ref, True)

    for h in range(heads):
        acc = acc_ref[h]
        o_ref[0, h * V_HEAD_DIM:(h + 1) * V_HEAD_DIM, :] = (
            acc[:V_HEAD_DIM] / acc[V_HEAD_DIM:V_HEAD_DIM + 1]).astype(BF16)


def _attention(q, k, vt, *, tq):
    bsz, seq, _ = q.shape
    heads_per_step = LANES // V_HEAD_DIM
    n_hp = N_HEADS // heads_per_step
    qk_w = heads_per_step * HEAD_PAD
    kern = functools.partial(_attn_kernel, tq=tq)
    return pl.pallas_call(
        kern,
        grid=(bsz, n_hp, seq // tq),
        in_specs=[pl.BlockSpec((1, tq, qk_w), lambda b, h, i: (b, i, h)),
                  pl.BlockSpec((1, seq, qk_w), lambda b, h, i: (b, 0, h)),
                  pl.BlockSpec((1, LANES, seq), lambda b, h, i: (b, h, 0))],
        out_specs=pl.BlockSpec((1, LANES, tq), lambda b, h, i: (b, h, i)),
        out_shape=jax.ShapeDtypeStruct((bsz, N_HEADS * V_HEAD_DIM, seq), BF16),
        scratch_shapes=[pltpu.VMEM((heads_per_step, 1, tq), F32),
                        pltpu.VMEM((heads_per_step, V_HEAD_DIM + ONES_ROWS, tq), F32),
                        pltpu.VMEM((heads_per_step, tq, tq), F32), pltpu.VMEM((heads_per_step, tq, tq), F32)],
        compiler_params=pltpu.CompilerParams(dimension_semantics=("arbitrary", "arbitrary", "arbitrary"),
                                             vmem_limit_bytes=VMEM_LIMIT_BYTES),
        name="mla_attention",
    )(q, k, vt)


def _post_attn_kernel(x_ref, attn_ref, ga_ref, gby_ref, mod_ref, wo_ref, wout_ref, gffn_ref, wgu_ref, wdown_ref,
                      gfin_ref, o_ref, *, ffn_chunk, final):
    x = x_ref[0]
    gt1 = mod_ref[0, 0, 2]
    sh2 = mod_ref[0, 0, 3]
    sc2 = mod_ref[0, 0, 4]
    gt2 = mod_ref[0, 0, 5]
    y_attn = lax.dot_general(attn_ref[0], wo_ref[0], (((0,), (0,)), ((), ())), preferred_element_type=F32)
    y = ga_ref[0].astype(F32) * y_attn + gby_ref[0].astype(F32)
    x1 = x + gt1 * _dot(y.astype(BF16), wout_ref[0])

    hb = (_rms(x1, gffn_ref[0]) * (1.0 + sc2) + sh2).astype(BF16)
    hidden = wdown_ref.shape[1]
    ffn = None
    for c0 in range(0, hidden, ffn_chunk):
        c1 = min(c0 + ffn_chunk, hidden)
        g = _dot(hb, wgu_ref[0, :, c0:c1])
        up = _dot(hb, wgu_ref[0, :, hidden + c0:hidden + c1])
        part = _dot((g * jax.nn.sigmoid(g) * up).astype(BF16), wdown_ref[0, c0:c1, :])
        ffn = part if ffn is None else ffn + part
    x2 = x1 + gt2 * ffn
    if final:
        x2 = _rms(x2, gfin_ref[...])
    o_ref[0] = x2


def _post_attention(layer, x, attn, ga, gby, mod, p, g_final, *, tm, final):
    bsz, seq, d_model = x.shape
    l = layer
    tok = lambda w: pl.BlockSpec((1, tm, w), lambda b, s: (b, s, 0))
    wspec = lambda a: _const_spec((1,) + a.shape[1:], lambda b, s: (l,) + (0,) * (a.ndim - 1))
    weights = [p["w_o"], p["w_out"], p["g_ffn"], p["w_gu"], p["w_down"]]
    kern = functools.partial(_post_attn_kernel, ffn_chunk=512, final=final)
    return pl.pallas_call(
        kern,
        grid=(bsz, seq // tm),
        in_specs=[tok(d_model), pl.BlockSpec((1, attn.shape[1], tm), lambda b, s: (b, 0, s)), tok(d_model),
                  tok(d_model),
                  pl.BlockSpec((1, 1, N_MOD, 1, d_model), lambda b, s: (l, b, 0, 0, 0))]
                 + [wspec(w) for w in weights]
                 + [_const_spec((1, d_model), lambda b, s: (0, 0))],
        out_specs=tok(d_model),
        out_shape=jax.ShapeDtypeStruct((bsz, seq, d_model), F32),
        compiler_params=pltpu.CompilerParams(dimension_semantics=("arbitrary", "arbitrary"),
                                             vmem_limit_bytes=VMEM_LIMIT_BYTES),
        name="post_attention",
    )(x, attn, ga, gby, mod, *weights, g_final)


def _layout_params(w_in, g_mix, g_q, w_uq, g_kv, w_ukv, w_o_attn, w_dw, b_dw, g_cn, b_cn, w_pw2, w_out, g_ffn,
                   w_gu, w_down):
    depth, d_model, _ = w_in.shape
    q_rank, kv_rank, conv_ch = g_q.shape[-1], g_kv.shape[-1], g_cn.shape[-1]
    zeros = lambda *s: jnp.zeros((depth,) + s, F32)
    o = 0
    def take(width):
        nonlocal o
        piece = w_in[:, :, o:o + width]
        o += width
        return piece
    q_lat, kv_lat, k_rope = take(q_rank), take(kv_rank), take(QK_ROPE_DIM)
    glu, gate_a, gate_b = take(2 * conv_ch), take(d_model), take(d_model)
    kr1, kr2 = k_rope[..., :HALF_ROPE], k_rope[..., HALF_ROPE:]
    pad_lo, pad_hi = zeros(d_model, QK_NOPE_DIM), zeros(d_model, HEAD_PAD - QK_DIM)
    kr_pad = jnp.concatenate([pad_lo, kr1, kr2, pad_hi], -1)
    kr_swap = jnp.concatenate([pad_lo, kr2, kr1, pad_hi], -1)
    w_in_l = jnp.concatenate([q_lat, kv_lat, kr_pad, kr_swap, glu, gate_a, gate_b], -1).astype(BF16)

    uq = w_uq.reshape(depth, q_rank, N_HEADS, QK_DIM)
    nope, r1, r2 = uq[..., :QK_NOPE_DIM], uq[..., QK_NOPE_DIM:QK_NOPE_DIM + HALF_ROPE], uq[..., QK_NOPE_DIM + HALF_ROPE:]
    z_lo = zeros(q_rank, N_HEADS, QK_NOPE_DIM)
    z_hi = zeros(q_rank, N_HEADS, HEAD_PAD - QK_DIM)
    q_cat = jnp.concatenate([nope, r1, r2, z_hi], -1).reshape(depth, q_rank, N_HEADS * HEAD_PAD)
    q_swap = jnp.concatenate([z_lo, r2, r1, z_hi], -1).reshape(depth, q_rank, N_HEADS * HEAD_PAD)
    w_q = jnp.concatenate([q_cat, q_swap], -1).astype(BF16)

    ukv = w_ukv.reshape(depth, kv_rank, N_HEADS, QK_NOPE_DIM + V_HEAD_DIM)
    k_cat = jnp.concatenate([ukv[..., :QK_NOPE_DIM], zeros(kv_rank, N_HEADS, HEAD_PAD - QK_NOPE_DIM)], -1)
    w_k = k_cat.reshape(depth, kv_rank, N_HEADS * HEAD_PAD).astype(BF16)
    w_vt = jnp.swapaxes(ukv[..., QK_NOPE_DIM:].reshape(depth, kv_rank, N_HEADS * V_HEAD_DIM), 1, 2).astype(BF16)
    row = lambda a: a.reshape(depth, 1, a.shape[-1])
    return dict(w_in=w_in_l, g_mix=row(g_mix), g_q=row(g_q), w_q=w_q, g_kv=row(g_kv), w_k=w_k, w_vt=w_vt,
                w_o=w_o_attn.astype(BF16), w_dw=w_dw, b_dw=row(b_dw), g_cn=row(g_cn), b_cn=row(b_cn),
                w_pw2=w_pw2.astype(BF16), w_out=w_out.astype(BF16), g_ffn=row(g_ffn),
                w_gu=w_gu.astype(BF16), w_down=w_down.astype(BF16))


def kernel(x, c, positions, w_ada, b_ada, g_mix, w_in, g_q, w_uq, g_kv, w_ukv, w_o_attn, w_dw, b_dw, g_cn, b_cn,
           w_pw2, w_out, g_ffn, w_gu, w_down, g_final):
    bsz, seq, d_model = x.shape
    depth = w_in.shape[0]
    tm = min(seq, 512)
    tq = min(seq, 512)
    p = _layout_params(w_in, g_mix, g_q, w_uq, g_kv, w_ukv, w_o_attn, w_dw, b_dw, g_cn, b_cn, w_pw2, w_out, g_ffn,
                       w_gu, w_down)
    ct, st = _rope_tables(positions)
    mod = _modulation(c, w_ada, b_ada).reshape(depth, bsz, N_MOD, 1, d_model)
    g_fin = g_final.reshape(1, d_model)
    for layer in range(depth):
        q, k, v, ga, gby = _pre_attention(layer, x, mod, ct, st, p, tm=tm)
        attn = _attention(q, k, v, tq=tq)
        x = _post_attention(layer, x, attn, ga, gby, mod, p, g_fin, tm=tm, final=layer == depth - 1)
    return x
```

```python
import functools
import math

import jax
import jax.numpy as jnp
from jax import lax
from jax.experimental import pallas as pl
from jax.experimental.pallas import tpu as pltpu

N_HEADS = 8
QK_NOPE_DIM = 64
QK_ROPE_DIM = 32
QK_DIM = QK_NOPE_DIM + QK_ROPE_DIM
V_HEAD_DIM = 64
ROPE_THETA = 10000.0
CONV_WIDTH = 31
N_MOD = 6
EPS = 1e-6
NEG_INF = -1e30

LANES = 128
HEAD_PAD = LANES
HALF_ROPE = QK_ROPE_DIM // 2
ONES_ROWS = 16
CONV_HALO = 32
CONV_ROW_BLOCK = 128
SUBLANES = 8
VMEM_LIMIT_BYTES = 56 * 1024 * 1024

BF16 = jnp.bfloat16
F32 = jnp.float32


def _dot(a, b):
    return jnp.dot(a, b, preferred_element_type=F32)


def _rms(x, gain):
    return x * lax.rsqrt(jnp.mean(x * x, axis=-1, keepdims=True) + EPS) * gain


def _const_spec(shape, index_map):
    return pl.BlockSpec(shape, index_map, pipeline_mode=pl.Buffered(1))


def _rope_table_kernel(pos_ref, invf_ref, sign_ref, ct_ref, st_ref):
    ang = pos_ref[...].astype(F32) * invf_ref[...]
    ct_ref[...] = jnp.cos(ang)
    st_ref[...] = jnp.sin(ang) * sign_ref[...]


def _rope_tables(positions):
    n_tok = positions.size
    tt = min(n_tok, 2048)
    inv_freq = ROPE_THETA ** (-jnp.arange(0, QK_ROPE_DIM, 2, dtype=F32) / QK_ROPE_DIM)
    zeros = lambda n: jnp.zeros((n,), F32)
    invf = jnp.concatenate([zeros(QK_NOPE_DIM), inv_freq, inv_freq, zeros(HEAD_PAD - QK_DIM)])[None, :]
    sign = jnp.concatenate([zeros(QK_NOPE_DIM), -jnp.ones((HALF_ROPE,), F32), jnp.ones((HALF_ROPE,), F32),
                            zeros(HEAD_PAD - QK_DIM)])[None, :]
    pos_b = jnp.broadcast_to(positions.reshape(n_tok, 1), (n_tok, LANES))
    row = pl.BlockSpec((1, LANES), lambda i: (0, 0))
    tile = pl.BlockSpec((tt, LANES), lambda i: (i, 0))
    return pl.pallas_call(
        _rope_table_kernel,
        grid=(n_tok // tt,),
        in_specs=[tile, row, row],
        out_specs=[tile, tile],
        out_shape=[jax.ShapeDtypeStruct((n_tok, LANES), F32)] * 2,
        name="rope_tables",
    )(pos_b, invf, sign)


def _mod_kernel(c_ref, w_ref, b_ref, o_ref):
    c = c_ref[...]
    c_act = (c * jax.nn.sigmoid(c)).astype(BF16)
    o_ref[0] = _dot(c_act, w_ref[0].astype(BF16)) + b_ref[0]


def _modulation(c, w_ada, b_ada):
    depth, d_model, n_out = w_ada.shape
    bsz = c.shape[0]
    tn = d_model
    return pl.pallas_call(
        _mod_kernel,
        grid=(depth, n_out // tn),
        in_specs=[pl.BlockSpec((bsz, d_model), lambda l, j: (0, 0)),
                  pl.BlockSpec((1, d_model, tn), lambda l, j: (l, 0, j)),
                  pl.BlockSpec((1, 1, tn), lambda l, j: (l, 0, j))],
        out_specs=pl.BlockSpec((1, bsz, tn), lambda l, j: (l, 0, j)),
        out_shape=jax.ShapeDtypeStruct((depth, bsz, n_out), F32),
        name="adaln_modulation",
    )(c, w_ada, b_ada.reshape(depth, 1, n_out))


def _pre_attn_kernel(x_ref, mod_ref, gmix_ref, ct_ref, st_ref, win_ref, gq_ref, wq_ref, gkv_ref, wk_ref, wvt_ref,
                     wdw_ref, bdw_ref, gcn_ref, bcn_ref, wpw2_ref,
                     q_out, k_out, vt_out, ga_out, gby_out, uext_ref, conv_ref, *, tm, dims):
    q_rank, kv_rank, conv_ch, d_model = dims

    @pl.when(pl.program_id(1) == 0)
    def _():
        uext_ref[0:CONV_HALO, :] = jnp.zeros((CONV_HALO, conv_ch), F32)

    x = x_ref[0]
    sh1 = mod_ref[0, 0, 0]
    sc1 = mod_ref[0, 0, 1]
    hb = (_rms(x, gmix_ref[0]) * (1.0 + sc1) + sh1).astype(BF16)

    widths = dict(q_lat=q_rank, kv_lat=kv_rank, kr=HEAD_PAD, kr_swap=HEAD_PAD, glu_a=conv_ch, glu_b=conv_ch,
                  gate_a=d_model, gate_b=d_model)
    starts, c0 = {}, 0
    for name, width in widths.items():
        starts[name] = c0
        c0 += width

    def proj(name):
        return _dot(hb, win_ref[0, :, starts[name]:starts[name] + widths[name]])

    uext_ref[CONV_HALO:CONV_HALO + tm, :] = proj("glu_a") * jax.nn.sigmoid(proj("glu_b"))
    first = CONV_HALO - (CONV_WIDTH - 1)
    win_rows = CONV_ROW_BLOCK + CONV_HALO
    for c in range(conv_ch // LANES):
        cl = slice(c * LANES, (c + 1) * LANES)
        for rb in range(tm // CONV_ROW_BLOCK):
            r0 = rb * CONV_ROW_BLOCK
            window = uext_ref[r0:r0 + win_rows, cl]
            acc = jnp.broadcast_to(bdw_ref[0, :, cl], (CONV_ROW_BLOCK, LANES))
            for res in range(SUBLANES):
                shifted = window if res == 0 else pltpu.roll(window, win_rows - res, axis=0)
                for off in range(res, CONV_HALO + 1, SUBLANES):
                    k = off - first
                    if 0 <= k < CONV_WIDTH:
                        acc = acc + wdw_ref[0, k:k + 1, cl] * shifted[off - res:off - res + CONV_ROW_BLOCK]
            conv_ref[r0:r0 + CONV_ROW_BLOCK, cl] = acc
    uext_ref[0:CONV_HALO, :] = uext_ref[tm:tm + CONV_HALO, :]

    ga_out[0] = jax.nn.sigmoid(proj("gate_a")).astype(BF16)

    ct = ct_ref[...]
    st = st_ref[...]
    q_scale = QK_DIM ** -0.5 * math.log2(math.e)
    ctq = ct * q_scale
    stq = st * q_scale
    n_qk = N_HEADS * HEAD_PAD

    qn = _rms(proj("q_lat"), gq_ref[0]).astype(BF16)
    q2 = _dot(qn, wq_ref[0])
    for h in range(N_HEADS):
        lo, hi = h * HEAD_PAD, (h + 1) * HEAD_PAD
        q_out[0, :, lo:hi] = (q2[:, lo:hi] * ctq + q2[:, n_qk + lo:n_qk + hi] * stq).astype(BF16)

    kvn = _rms(proj("kv_lat"), gkv_ref[0]).astype(BF16)
    kr = proj("kr") * ct + proj("kr_swap") * st
    k2 = _dot(kvn, wk_ref[0])
    for h in range(N_HEADS):
        lo, hi = h * HEAD_PAD, (h + 1) * HEAD_PAD
        k_out[0, :, lo:hi] = (k2[:, lo:hi] + kr).astype(BF16)
    vt_out[0] = lax.dot_general(wvt_ref[0], kvn, (((1,), (1,)), ((), ())), preferred_element_type=F32).astype(BF16)

    gate_b = jax.nn.sigmoid(proj("gate_b"))
    u = conv_ref[...]
    mu = jnp.mean(u, axis=-1, keepdims=True)
    uc = u - mu
    un = uc * lax.rsqrt(jnp.mean(uc * uc, axis=-1, keepdims=True) + EPS) * gcn_ref[0] + bcn_ref[0]
    y_conv = _dot((un * jax.nn.sigmoid(un)).astype(BF16), wpw2_ref[0])
    gby_out[0] = (gate_b * y_conv).astype(BF16)


def _pre_attention(layer, x, mod, ct, st, p, *, tm):
    bsz, seq, d_model = x.shape
    q_rank, kv_rank, conv_ch = p["g_q"].shape[-1], p["g_kv"].shape[-1], p["g_cn"].shape[-1]
    n_qk = N_HEADS * HEAD_PAD
    n_v = N_HEADS * V_HEAD_DIM
    n_s = seq // tm
    l = layer
    tok = lambda w: pl.BlockSpec((1, tm, w), lambda b, s: (b, s, 0))
    wspec = lambda a: _const_spec((1,) + a.shape[1:], lambda b, s: (l,) + (0,) * (a.ndim - 1))
    tab = pl.BlockSpec((tm, LANES), lambda b, s: (b * n_s + s, 0))
    weights = [p["w_in"], p["g_q"], p["w_q"], p["g_kv"], p["w_k"], p["w_vt"], p["w_dw"], p["b_dw"], p["g_cn"],
               p["b_cn"], p["w_pw2"]]
    kern = functools.partial(_pre_attn_kernel, tm=tm, dims=(q_rank, kv_rank, conv_ch, d_model))
    out_bf16 = lambda w: jax.ShapeDtypeStruct((bsz, seq, w), BF16)
    return pl.pallas_call(
        kern,
        grid=(bsz, n_s),
        in_specs=[tok(d_model),
                  pl.BlockSpec((1, 1, N_MOD, 1, d_model), lambda b, s: (l, b, 0, 0, 0)),
                  wspec(p["g_mix"]), tab, tab] + [wspec(w) for w in weights],
        out_specs=[tok(n_qk), tok(n_qk), pl.BlockSpec((1, n_v, tm), lambda b, s: (b, 0, s)), tok(d_model),
                   tok(d_model)],
        out_shape=[out_bf16(n_qk), out_bf16(n_qk), jax.ShapeDtypeStruct((bsz, n_v, seq), BF16), out_bf16(d_model),
                   out_bf16(d_model)],
        scratch_shapes=[pltpu.VMEM((CONV_HALO + tm, conv_ch), F32), pltpu.VMEM((tm, conv_ch), F32)],
        compiler_params=pltpu.CompilerParams(dimension_semantics=("arbitrary", "arbitrary"),
                                             vmem_limit_bytes=VMEM_LIMIT_BYTES),
        name="pre_attention",
    )(x, mod, p["g_mix"], ct, st, *weights)


def _attn_kernel(q_ref, k_ref, vt_ref, o_ref, m_ref, acc_ref, sa_ref, sb_ref, *, tq):
    qi = pl.program_id(2)
    heads = q_ref.shape[-1] // HEAD_PAD
    m_ref[...] = jnp.full(m_ref.shape, NEG_INF, F32)
    acc_ref[...] = jnp.zeros(acc_ref.shape, F32)
    ones_rows = jnp.ones((ONES_ROWS, tq), BF16)

    def key_rows(j):
        return pl.ds(pl.multiple_of(j * tq, tq), tq)

    def scores(j, s_ref):
        for h in range(heads):
            hl = slice(h * HEAD_PAD, (h + 1) * HEAD_PAD)
            s_ref[h] = lax.dot_general(k_ref[0, key_rows(j), hl], q_ref[0, :, hl], (((1,), (1,)), ((), ())),
                                       preferred_element_type=F32)

    def update(j, s_ref, masked):
        for h in range(heads):
            vl = slice(h * V_HEAD_DIM, (h + 1) * V_HEAD_DIM)
            s_t = s_ref[h]
            if masked:
                key_id = lax.broadcasted_iota(jnp.int32, s_t.shape, 0)
                qry_id = lax.broadcasted_iota(jnp.int32, s_t.shape, 1)
                s_t = jnp.where(key_id <= qry_id, s_t, NEG_INF)
            m_prev = m_ref[h]
            m_new = jnp.maximum(m_prev, jnp.max(s_t, axis=0, keepdims=True))
            alpha = jnp.exp2(m_prev - m_new)
            p_t = jnp.exp2(s_t - m_new).astype(BF16)
            v_ext = jnp.concatenate([vt_ref[0, vl, key_rows(j)], ones_rows], axis=0)
            acc_ref[h] = alpha * acc_ref[h] + _dot(v_ext, p_t)
            m_ref[h] = m_new

    scores(0, sa_ref)

    def pair(t, carry):
        j = 2 * t
        scores(j + 1, sb_ref)
        update(j, sa_ref, False)
        scores(j + 2, sa_ref)
        update(j + 1, sb_ref, False)
        return carry

    lax.fori_loop(0, qi // 2, pair, 0)
    odd = qi % 2 == 1

    @pl.when(odd)
    def _():
        scores(qi, sb_ref)
        update(qi - 1, sa_ref, False)
        update(qi, sb_ref, True)

    @pl.when(jnp.logical_not(odd))
    def _():
        update(qi, sa_ref, True)

    for h in range(heads):
        acc = acc_ref[h]
        o_ref[0, h * V_HEAD_DIM:(h + 1) * V_HEAD_DIM, :] = (
            acc[:V_HEAD_DIM] / acc[V_HEAD_DIM:V_HEAD_DIM + 1]).astype(BF16)


def _attention(q, k, vt, *, tq):
    bsz, seq, _ = q.shape
    heads_per_step = LANES // V_HEAD_DIM
    n_hp = N_HEADS // heads_per_step
    qk_w = heads_per_step * HEAD_PAD
    kern = functools.partial(_attn_kernel, tq=tq)
    return pl.pallas_call(
        kern,
        grid=(bsz, n_hp, seq // tq),
        in_specs=[pl.BlockSpec((1, tq, qk_w), lambda b, h, i: (b, i, h)),
                  pl.BlockSpec((1, seq, qk_w), lambda b, h, i: (b, 0, h)),
                  pl.BlockSpec((1, LANES, seq), lambda b, h, i: (b, h, 0))],
        out_specs=pl.BlockSpec((1, LANES, tq), lambda b, h, i: (b, h, i)),
        out_shape=jax.ShapeDtypeStruct((bsz, N_HEADS * V_HEAD_DIM, seq), BF16),
        scratch_shapes=[pltpu.VMEM((heads_per_step, 1, tq), F32),
                        pltpu.VMEM((heads_per_step, V_HEAD_DIM + ONES_ROWS, tq), F32),
                        pltpu.VMEM((heads_per_step, tq, tq), F32), pltpu.VMEM((heads_per_step, tq, tq), F32)],
        compiler_params=pltpu.CompilerParams(dimension_semantics=("arbitrary", "arbitrary", "arbitrary"),
                                             vmem_limit_bytes=VMEM_LIMIT_BYTES),
        name="mla_attention",
    )(q, k, vt)


def _post_attn_kernel(x_ref, attn_ref, ga_ref, gby_ref, mod_ref, wo_ref, wout_ref, gffn_ref, wgu_ref, wdown_ref,
                      gfin_ref, o_ref, *, ffn_chunk, final):
    x = x_ref[0]
    gt1 = mod_ref[0, 0, 2]
    sh2 = mod_ref[0, 0, 3]
    sc2 = mod_ref[0, 0, 4]
    gt2 = mod_ref[0, 0, 5]
    y_attn = lax.dot_general(attn_ref[0], wo_ref[0], (((0,), (0,)), ((), ())), preferred_element_type=F32)
    y = ga_ref[0].astype(F32) * y_attn + gby_ref[0].astype(F32)
    x1 = x + gt1 * _dot(y.astype(BF16), wout_ref[0])

    hb = (_rms(x1, gffn_ref[0]) * (1.0 + sc2) + sh2).astype(BF16)
    hidden = wdown_ref.shape[1]
    ffn = None
    for c0 in range(0, hidden, ffn_chunk):
        c1 = min(c0 + ffn_chunk, hidden)
        g = _dot(hb, wgu_ref[0, :, c0:c1])
        up = _dot(hb, wgu_ref[0, :, hidden + c0:hidden + c1])
        part = _dot((g * jax.nn.sigmoid(g) * up).astype(BF16), wdown_ref[0, c0:c1, :])
        ffn = part if ffn is None else ffn + part
    x2 = x1 + gt2 * ffn
    if final:
        x2 = _rms(x2, gfin_ref[...])
    o_ref[0] = x2


def _post_attention(layer, x, attn, ga, gby, mod, p, g_final, *, tm, final):
    bsz, seq, d_model = x.shape
    l = layer
    tok = lambda w: pl.BlockSpec((1, tm, w), lambda b, s: (b, s, 0))
    wspec = lambda a: _const_spec((1,) + a.shape[1:], lambda b, s: (l,) + (0,) * (a.ndim - 1))
    weights = [p["w_o"], p["w_out"], p["g_ffn"], p["w_gu"], p["w_down"]]
    kern = functools.partial(_post_attn_kernel, ffn_chunk=512, final=final)
    return pl.pallas_call(
        kern,
        grid=(bsz, seq // tm),
        in_specs=[tok(d_model), pl.BlockSpec((1, attn.shape[1], tm), lambda b, s: (b, 0, s)), tok(d_model),
                  tok(d_model),
                  pl.BlockSpec((1, 1, N_MOD, 1, d_model), lambda b, s: (l, b, 0, 0, 0))]
                 + [wspec(w) for w in weights]
                 + [_const_spec((1, d_model), lambda b, s: (0, 0))],
        out_specs=tok(d_model),
        out_shape=jax.ShapeDtypeStruct((bsz, seq, d_model), F32),
        compiler_params=pltpu.CompilerParams(dimension_semantics=("arbitrary", "arbitrary"),
                                             vmem_limit_bytes=VMEM_LIMIT_BYTES),
        name="post_attention",
    )(x, attn, ga, gby, mod, *weights, g_final)


def _layout_params(w_in, g_mix, g_q, w_uq, g_kv, w_ukv, w_o_attn, w_dw, b_dw, g_cn, b_cn, w_pw2, w_out, g_ffn,
                   w_gu, w_down):
    depth, d_model, _ = w_in.shape
    q_rank, kv_rank, conv_ch = g_q.shape[-1], g_kv.shape[-1], g_cn.shape[-1]
    zeros = lambda *s: jnp.zeros((depth,) + s, F32)
    o = 0
    def take(width):
        nonlocal o
        piece = w_in[:, :, o:o + width]
        o += width
        return piece
    q_lat, kv_lat, k_rope = take(q_rank), take(kv_rank), take(QK_ROPE_DIM)
    glu, gate_a, gate_b = take(2 * conv_ch), take(d_model), take(d_model)
    kr1, kr2 = k_rope[..., :HALF_ROPE], k_rope[..., HALF_ROPE:]
    pad_lo, pad_hi = zeros(d_model, QK_NOPE_DIM), zeros(d_model, HEAD_PAD - QK_DIM)
    kr_pad = jnp.concatenate([pad_lo, kr1, kr2, pad_hi], -1)
    kr_swap = jnp.concatenate([pad_lo, kr2, kr1, pad_hi], -1)
    w_in_l = jnp.concatenate([q_lat, kv_lat, kr_pad, kr_swap, glu, gate_a, gate_b], -1).astype(BF16)

    uq = w_uq.reshape(depth, q_rank, N_HEADS, QK_DIM)
    nope, r1, r2 = uq[..., :QK_NOPE_DIM], uq[..., QK_NOPE_DIM:QK_NOPE_DIM + HALF_ROPE], uq[..., QK_NOPE_DIM + HALF_ROPE:]
    z_lo = zeros(q_rank, N_HEADS, QK_NOPE_DIM)
    z_hi = zeros(q_rank, N_HEADS, HEAD_PAD - QK_DIM)
    q_cat = jnp.concatenate([nope, r1, r2, z_hi], -1).reshape(depth, q_rank, N_HEADS * HEAD_PAD)
    q_swap = jnp.concatenate([z_lo, r2, r1, z_hi], -1).reshape(depth, q_rank, N_HEADS * HEAD_PAD)
    w_q = jnp.concatenate([q_cat, q_swap], -1).astype(BF16)

    ukv = w_ukv.reshape(depth, kv_rank, N_HEADS, QK_NOPE_DIM + V_HEAD_DIM)
    k_cat = jnp.concatenate([ukv[..., :QK_NOPE_DIM], zeros(kv_rank, N_HEADS, HEAD_PAD - QK_NOPE_DIM)], -1)
    w_k = k_cat.reshape(depth, kv_rank, N_HEADS * HEAD_PAD).astype(BF16)
    w_vt = jnp.swapaxes(ukv[..., QK_NOPE_DIM:].reshape(depth, kv_rank, N_HEADS * V_HEAD_DIM), 1, 2).astype(BF16)
    row = lambda a: a.reshape(depth, 1, a.shape[-1])
    return dict(w_in=w_in_l, g_mix=row(g_mix), g_q=row(g_q), w_q=w_q, g_kv=row(g_kv), w_k=w_k, w_vt=w_vt,
                w_o=w_o_attn.astype(BF16), w_dw=w_dw, b_dw=row(b_dw), g_cn=row(g_cn), b_cn=row(b_cn),
                w_pw2=w_pw2.astype(BF16), w_out=w_out.astype(BF16), g_ffn=row(g_ffn),
                w_gu=w_gu.astype(BF16), w_down=w_down.astype(BF16))


def kernel(x, c, positions, w_ada, b_ada, g_mix, w_in, g_q, w_uq, g_kv, w_ukv, w_o_attn, w_dw, b_dw, g_cn, b_cn,
           w_pw2, w_out, g_ffn, w_gu, w_down, g_final):
    bsz, seq, d_model = x.shape
    depth = w_in.shape[0]
    tm = min(seq, 512)
    tq = min(seq, 512)
    p = _layout_params(w_in, g_mix, g_q, w_uq, g_kv, w_ukv, w_o_attn, w_dw, b_dw, g_cn, b_cn, w_pw2, w_out, g_ffn,
                       w_gu, w_down)
    ct, st = _rope_tables(positions)
    mod = _modulation(c, w_ada, b_ada).reshape(depth, bsz, N_MOD, 1, d_model)
    g_fin = g_final.reshape(1, d_model)
    for layer in range(depth):
        q, k, v, ga, gby = _pre_attention(layer, x, mod, ct, st, p, tm=tm)
        attn = _attention(q, k, v, tq=tq)
        x = _post_attention(layer, x, attn, ga, gby, mod, p, g_fin, tm=tm, final=layer == depth - 1)
    return x
```

```python
import functools
import math

import jax
import jax.numpy as jnp
from jax import lax
from jax.experimental import pallas as pl
from jax.experimental.pallas import tpu as pltpu

N_HEADS = 8
QK_NOPE_DIM = 64
QK_ROPE_DIM = 32
QK_DIM = QK_NOPE_DIM + QK_ROPE_DIM
V_HEAD_DIM = 64
ROPE_THETA = 10000.0
CONV_WIDTH = 31
N_MOD = 6
EPS = 1e-6
NEG_INF = -1e30

LANES = 128
HEAD_PAD = LANES
HALF_ROPE = QK_ROPE_DIM // 2
ONES_ROWS = 16
CONV_HALO = 32
CONV_ROW_BLOCK = 128
SUBLANES = 8
VMEM_LIMIT_BYTES = 56 * 1024 * 1024

BF16 = jnp.bfloat16
F32 = jnp.float32


def _dot(a, b):
    return jnp.dot(a, b, preferred_element_type=F32)


def _rms(x, gain):
    return x * lax.rsqrt(jnp.mean(x * x, axis=-1, keepdims=True) + EPS) * gain


def _swap_halves(slab):
    return pltpu.roll(slab, HALF_ROPE, axis=1) + pltpu.roll(slab, HEAD_PAD - HALF_ROPE, axis=1)


def _const_spec(shape, index_map):
    return pl.BlockSpec(shape, index_map, pipeline_mode=pl.Buffered(1))


def _rope_table_kernel(pos_ref, invf_ref, sign_ref, ct_ref, st_ref):
    ang = pos_ref[...].astype(F32) * invf_ref[...]
    ct_ref[...] = jnp.cos(ang)
    st_ref[...] = jnp.sin(ang) * sign_ref[...]


def _rope_tables(positions):
    n_tok = positions.size
    tt = min(n_tok, 2048)
    inv_freq = ROPE_THETA ** (-jnp.arange(0, QK_ROPE_DIM, 2, dtype=F32) / QK_ROPE_DIM)
    zeros = lambda n: jnp.zeros((n,), F32)
    guard, tail = zeros(HALF_ROPE), zeros(HALF_ROPE + QK_NOPE_DIM)
    invf = jnp.concatenate([guard, inv_freq, inv_freq, tail])[None, :]
    sign = jnp.concatenate([guard, -jnp.ones((HALF_ROPE,), F32), jnp.ones((HALF_ROPE,), F32), tail])[None, :]
    pos_b = jnp.broadcast_to(positions.reshape(n_tok, 1), (n_tok, LANES))
    row = pl.BlockSpec((1, LANES), lambda i: (0, 0))
    tile = pl.BlockSpec((tt, LANES), lambda i: (i, 0))
    return pl.pallas_call(
        _rope_table_kernel,
        grid=(n_tok // tt,),
        in_specs=[tile, row, row],
        out_specs=[tile, tile],
        out_shape=[jax.ShapeDtypeStruct((n_tok, LANES), F32)] * 2,
        name="rope_tables",
    )(pos_b, invf, sign)


def _mod_kernel(c_ref, w_ref, b_ref, o_ref):
    c = c_ref[...]
    c_act = (c * jax.nn.sigmoid(c)).astype(BF16)
    o_ref[0] = _dot(c_act, w_ref[0].astype(BF16)) + b_ref[0]


def _modulation(c, w_ada, b_ada):
    depth, d_model, n_out = w_ada.shape
    bsz = c.shape[0]
    tn = d_model
    return pl.pallas_call(
        _mod_kernel,
        grid=(depth, n_out // tn),
        in_specs=[pl.BlockSpec((bsz, d_model), lambda l, j: (0, 0)),
                  pl.BlockSpec((1, d_model, tn), lambda l, j: (l, 0, j)),
                  pl.BlockSpec((1, 1, tn), lambda l, j: (l, 0, j))],
        out_specs=pl.BlockSpec((1, bsz, tn), lambda l, j: (l, 0, j)),
        out_shape=jax.ShapeDtypeStruct((depth, bsz, n_out), F32),
        name="adaln_modulation",
    )(c, w_ada, b_ada.reshape(depth, 1, n_out))


def _pre_attn_kernel(x_ref, mod_ref, gmix_ref, ct_ref, st_ref, win_ref, gq_ref, wq_ref, gkv_ref, wk_ref, wvt_ref,
                     wdw_ref, bdw_ref, gcn_ref, bcn_ref, wpw2_ref,
                     q_out, k_out, vt_out, ga_out, gby_out, uext_ref, conv_ref, *, tm, dims):
    q_rank, kv_rank, conv_ch, d_model = dims

    @pl.when(pl.program_id(1) == 0)
    def _():
        uext_ref[0:CONV_HALO, :] = jnp.zeros((CONV_HALO, conv_ch), F32)

    x = x_ref[0]
    sh1 = mod_ref[0, 0, 0]
    sc1 = mod_ref[0, 0, 1]
    hb = (_rms(x, gmix_ref[0]) * (1.0 + sc1) + sh1).astype(BF16)

    widths = dict(q_lat=q_rank, kv_lat=kv_rank, kr=HEAD_PAD, glu_a=conv_ch, glu_b=conv_ch,
                  gate_a=d_model, gate_b=d_model)
    starts, c0 = {}, 0
    for name, width in widths.items():
        starts[name] = c0
        c0 += width

    def proj(name):
        return _dot(hb, win_ref[0, :, starts[name]:starts[name] + widths[name]])

    uext_ref[CONV_HALO:CONV_HALO + tm, :] = proj("glu_a") * jax.nn.sigmoid(proj("glu_b"))
    first = CONV_HALO - (CONV_WIDTH - 1)
    win_rows = CONV_ROW_BLOCK + CONV_HALO
    for c in range(conv_ch // LANES):
        cl = slice(c * LANES, (c + 1) * LANES)
        for rb in range(tm // CONV_ROW_BLOCK):
            r0 = rb * CONV_ROW_BLOCK
            window = uext_ref[r0:r0 + win_rows, cl]
            acc = jnp.broadcast_to(bdw_ref[0, :, cl], (CONV_ROW_BLOCK, LANES))
            for res in range(SUBLANES):
                shifted = window if res == 0 else pltpu.roll(window, win_rows - res, axis=0)
                for off in range(res, CONV_HALO + 1, SUBLANES):
                    k = off - first
                    if 0 <= k < CONV_WIDTH:
                        acc = acc + wdw_ref[0, k:k + 1, cl] * shifted[off - res:off - res + CONV_ROW_BLOCK]
            conv_ref[r0:r0 + CONV_ROW_BLOCK, cl] = acc
    uext_ref[0:CONV_HALO, :] = uext_ref[tm:tm + CONV_HALO, :]

    ga_out[0] = jax.nn.sigmoid(proj("gate_a")).astype(BF16)

    ct = ct_ref[...]
    st = st_ref[...]
    q_scale = QK_DIM ** -0.5 * math.log2(math.e)
    ctq = ct * q_scale
    stq = st * q_scale

    qn = _rms(proj("q_lat"), gq_ref[0]).astype(BF16)
    q2 = _dot(qn, wq_ref[0])
    for h in range(N_HEADS):
        lo, hi = h * HEAD_PAD, (h + 1) * HEAD_PAD
        q_out[0, :, lo:hi] = (q2[:, lo:hi] * ctq + _swap_halves(q2[:, lo:hi]) * stq).astype(BF16)

    kvn = _rms(proj("kv_lat"), gkv_ref[0]).astype(BF16)
    kr = proj("kr")
    kr = kr * ct + _swap_halves(kr) * st
    k2 = _dot(kvn, wk_ref[0])
    for h in range(N_HEADS):
        lo, hi = h * HEAD_PAD, (h + 1) * HEAD_PAD
        k_out[0, :, lo:hi] = (k2[:, lo:hi] + kr).astype(BF16)
    vt_out[0] = lax.dot_general(wvt_ref[0], kvn, (((1,), (1,)), ((), ())), preferred_element_type=F32).astype(BF16)

    gate_b = jax.nn.sigmoid(proj("gate_b"))
    u = conv_ref[...]
    mu = jnp.mean(u, axis=-1, keepdims=True)
    uc = u - mu
    un = uc * lax.rsqrt(jnp.mean(uc * uc, axis=-1, keepdims=True) + EPS) * gcn_ref[0] + bcn_ref[0]
    y_conv = _dot((un * jax.nn.sigmoid(un)).astype(BF16), wpw2_ref[0])
    gby_out[0] = (gate_b * y_conv).astype(BF16)


def _pre_attention(layer, x, mod, ct, st, p, *, tm):
    bsz, seq, d_model = x.shape
    q_rank, kv_rank, conv_ch = p["g_q"].shape[-1], p["g_kv"].shape[-1], p["g_cn"].shape[-1]
    n_qk = N_HEADS * HEAD_PAD
    n_v = N_HEADS * V_HEAD_DIM
    n_s = seq // tm
    l = layer
    tok = lambda w: pl.BlockSpec((1, tm, w), lambda b, s: (b, s, 0))
    wspec = lambda a: _const_spec((1,) + a.shape[1:], lambda b, s: (l,) + (0,) * (a.ndim - 1))
    tab = pl.BlockSpec((tm, LANES), lambda b, s: (b * n_s + s, 0))
    weights = [p["w_in"], p["g_q"], p["w_q"], p["g_kv"], p["w_k"], p["w_vt"], p["w_dw"], p["b_dw"], p["g_cn"],
               p["b_cn"], p["w_pw2"]]
    kern = functools.partial(_pre_attn_kernel, tm=tm, dims=(q_rank, kv_rank, conv_ch, d_model))
    out_bf16 = lambda w: jax.ShapeDtypeStruct((bsz, seq, w), BF16)
    return pl.pallas_call(
        kern,
        grid=(bsz, n_s),
        in_specs=[tok(d_model),
                  pl.BlockSpec((1, 1, N_MOD, 1, d_model), lambda b, s: (l, b, 0, 0, 0)),
                  wspec(p["g_mix"]), tab, tab] + [wspec(w) for w in weights],
        out_specs=[tok(n_qk), tok(n_qk), pl.BlockSpec((1, n_v, tm), lambda b, s: (b, 0, s)), tok(d_model),
                   tok(d_model)],
        out_shape=[out_bf16(n_qk), out_bf16(n_qk), jax.ShapeDtypeStruct((bsz, n_v, seq), BF16), out_bf16(d_model),
                   out_bf16(d_model)],
        scratch_shapes=[pltpu.VMEM((CONV_HALO + tm, conv_ch), F32), pltpu.VMEM((tm, conv_ch), F32)],
        compiler_params=pltpu.CompilerParams(dimension_semantics=("arbitrary", "arbitrary"),
                                             vmem_limit_bytes=VMEM_LIMIT_BYTES),
        name="pre_attention",
    )(x, mod, p["g_mix"], ct, st, *weights)


def _attn_kernel(q_ref, k_ref, vt_ref, o_ref, m_ref, acc_ref, sa_ref, sb_ref, *, tq):
    heads = q_ref.shape[-1] // HEAD_PAD
    n_blocks = q_ref.shape[1] // tq
    half = tq // 2
    bufs = (sa_ref, sb_ref)
    steps = []
    for qi in range(n_blocks):
        for j in range(qi):
            steps.append([(qi, 0, tq, j * tq, (j + 1) * tq)])
        steps.append([(qi, 0, half, qi * tq, qi * tq + half), (qi, half, tq, qi * tq, (qi + 1) * tq)])

    def scores(step, s_ref):
        for qi, c0, c1, k0, k1 in step:
            for h in range(heads):
                hl = slice(h * HEAD_PAD, (h + 1) * HEAD_PAD)
                s_ref[h, 0:k1 - k0, c0:c1] = lax.dot_general(
                    k_ref[0, k0:k1, hl], q_ref[0, qi * tq + c0:qi * tq + c1, hl], (((1,), (1,)), ((), ())),
                    preferred_element_type=F32)

    def update(step, s_ref):
        for qi, c0, c1, k0, k1 in step:
            diagonal = k1 > qi * tq
            for h in range(heads):
                vl = slice(h * V_HEAD_DIM, (h + 1) * V_HEAD_DIM)
                s_t = s_ref[h, 0:k1 - k0, c0:c1]
                if diagonal:
                    r0 = qi * tq + c0 - k0
                    tri = s_t[r0:]
                    key_id = lax.broadcasted_iota(jnp.int32, tri.shape, 0)
                    qry_id = lax.broadcasted_iota(jnp.int32, tri.shape, 1)
                    tri = jnp.where(key_id <= qry_id, tri, NEG_INF)
                    s_t = tri if r0 == 0 else jnp.concatenate([s_t[:r0], tri], axis=0)
                m_new = jnp.max(s_t, axis=0, keepdims=True)
                if k0 > 0:
                    m_prev = m_ref[h, :, c0:c1]
                    m_new = jnp.maximum(m_prev, m_new)
                p_t = jnp.exp2(s_t - m_new).astype(BF16)
                v_ext = jnp.concatenate([vt_ref[0, vl, k0:k1], jnp.ones((ONES_ROWS, k1 - k0), BF16)], axis=0)
                acc = _dot(v_ext, p_t)
                if k0 > 0:
                    acc = jnp.exp2(m_prev - m_new) * acc_ref[h, :, c0:c1] + acc
                if not diagonal:
                    acc_ref[h, :, c0:c1] = acc
                    m_ref[h, :, c0:c1] = m_new
                else:
                    o_ref[0, vl, qi * tq + c0:qi * tq + c1] = (
                        acc[:V_HEAD_DIM] / acc[V_HEAD_DIM:V_HEAD_DIM + 1]).astype(BF16)

    scores(steps[0], bufs[0])
    for n, step in enumerate(steps):
        if n + 1 < len(steps):
            scores(steps[n + 1], bufs[(n + 1) % 2])
        update(step, bufs[n % 2])


def _attention(q, k, vt, *, tq):
    bsz, seq, _ = q.shape
    heads_per_step = LANES // V_HEAD_DIM
    n_hp = N_HEADS // heads_per_step
    qk_w = heads_per_step * HEAD_PAD
    kern = functools.partial(_attn_kernel, tq=tq)
    return pl.pallas_call(
        kern,
        grid=(bsz, n_hp),
        in_specs=[pl.BlockSpec((1, seq, qk_w), lambda b, h: (b, 0, h)),
                  pl.BlockSpec((1, seq, qk_w), lambda b, h: (b, 0, h)),
                  pl.BlockSpec((1, LANES, seq), lambda b, h: (b, h, 0))],
        out_specs=pl.BlockSpec((1, LANES, seq), lambda b, h: (b, h, 0)),
        out_shape=jax.ShapeDtypeStruct((bsz, N_HEADS * V_HEAD_DIM, seq), BF16),
        scratch_shapes=[pltpu.VMEM((heads_per_step, 1, tq), F32),
                        pltpu.VMEM((heads_per_step, V_HEAD_DIM + ONES_ROWS, tq), F32),
                        pltpu.VMEM((heads_per_step, tq, tq), F32), pltpu.VMEM((heads_per_step, tq, tq), F32)],
        compiler_params=pltpu.CompilerParams(dimension_semantics=("arbitrary", "arbitrary"),
                                             vmem_limit_bytes=VMEM_LIMIT_BYTES),
        name="mla_attention",
    )(q, k, vt)


def _post_attn_kernel(x_ref, attn_ref, ga_ref, gby_ref, mod_ref, wo_ref, wout_ref, gffn_ref, wgu_ref, wdown_ref,
                      gfin_ref, o_ref, *, ffn_chunk, final):
    x = x_ref[0]
    gt1 = mod_ref[0, 0, 2]
    sh2 = mod_ref[0, 0, 3]
    sc2 = mod_ref[0, 0, 4]
    gt2 = mod_ref[0, 0, 5]
    y_attn = lax.dot_general(attn_ref[0], wo_ref[0], (((0,), (0,)), ((), ())), preferred_element_type=F32)
    y = ga_ref[0].astype(F32) * y_attn + gby_ref[0].astype(F32)
    x1 = x + gt1 * _dot(y.astype(BF16), wout_ref[0])

    hb = (_rms(x1, gffn_ref[0]) * (1.0 + sc2) + sh2).astype(BF16)
    hidden = wdown_ref.shape[1]
    ffn = None
    for c0 in range(0, hidden, ffn_chunk):
        c1 = min(c0 + ffn_chunk, hidden)
        g = _dot(hb, wgu_ref[0, :, c0:c1])
        up = _dot(hb, wgu_ref[0, :, hidden + c0:hidden + c1])
        part = _dot((g * jax.nn.sigmoid(g) * up).astype(BF16), wdown_ref[0, c0:c1, :])
        ffn = part if ffn is None else ffn + part
    x2 = x1 + gt2 * ffn
    if final:
        x2 = _rms(x2, gfin_ref[...])
    o_ref[0] = x2


def _post_attention(layer, x, attn, ga, gby, mod, p, g_final, *, tm, final):
    bsz, seq, d_model = x.shape
    l = layer
    tok = lambda w: pl.BlockSpec((1, tm, w), lambda b, s: (b, s, 0))
    wspec = lambda a: _const_spec((1,) + a.shape[1:], lambda b, s: (l,) + (0,) * (a.ndim - 1))
    weights = [p["w_o"], p["w_out"], p["g_ffn"], p["w_gu"], p["w_down"]]
    kern = functools.partial(_post_attn_kernel, ffn_chunk=512, final=final)
    return pl.pallas_call(
        kern,
        grid=(bsz, seq // tm),
        in_specs=[tok(d_model), pl.BlockSpec((1, attn.shape[1], tm), lambda b, s: (b, 0, s)), tok(d_model),
                  tok(d_model),
                  pl.BlockSpec((1, 1, N_MOD, 1, d_model), lambda b, s: (l, b, 0, 0, 0))]
                 + [wspec(w) for w in weights]
                 + [_const_spec((1, d_model), lambda b, s: (0, 0))],
        out_specs=tok(d_model),
        out_shape=jax.ShapeDtypeStruct((bsz, seq, d_model), F32),
        compiler_params=pltpu.CompilerParams(dimension_semantics=("arbitrary", "arbitrary"),
                                             vmem_limit_bytes=VMEM_LIMIT_BYTES),
        name="post_attention",
    )(x, attn, ga, gby, mod, *weights, g_final)


def _layout_params(w_in, g_mix, g_q, w_uq, g_kv, w_ukv, w_o_attn, w_dw, b_dw, g_cn, b_cn, w_pw2, w_out, g_ffn,
                   w_gu, w_down):
    depth, d_model, _ = w_in.shape
    q_rank, kv_rank, conv_ch = g_q.shape[-1], g_kv.shape[-1], g_cn.shape[-1]
    zeros = lambda *s: jnp.zeros((depth,) + s, F32)
    o = 0
    def take(width):
        nonlocal o
        piece = w_in[:, :, o:o + width]
        o += width
        return piece
    q_lat, kv_lat, k_rope = take(q_rank), take(kv_rank), take(QK_ROPE_DIM)
    glu, gate_a, gate_b = take(2 * conv_ch), take(d_model), take(d_model)
    kr1, kr2 = k_rope[..., :HALF_ROPE], k_rope[..., HALF_ROPE:]
    kr_pad = jnp.concatenate([zeros(d_model, HALF_ROPE), kr1, kr2, zeros(d_model, HALF_ROPE + QK_NOPE_DIM)], -1)
    w_in_l = jnp.concatenate([q_lat, kv_lat, kr_pad, glu, gate_a, gate_b], -1).astype(BF16)

    uq = w_uq.reshape(depth, q_rank, N_HEADS, QK_DIM)
    nope, r1, r2 = uq[..., :QK_NOPE_DIM], uq[..., QK_NOPE_DIM:QK_NOPE_DIM + HALF_ROPE], uq[..., QK_NOPE_DIM + HALF_ROPE:]
    guard = zeros(q_rank, N_HEADS, HALF_ROPE)
    w_q = jnp.concatenate([guard, r1, r2, guard, nope], -1).reshape(depth, q_rank, N_HEADS * HEAD_PAD).astype(BF16)

    ukv = w_ukv.reshape(depth, kv_rank, N_HEADS, QK_NOPE_DIM + V_HEAD_DIM)
    k_cat = jnp.concatenate([zeros(kv_rank, N_HEADS, HEAD_PAD - QK_NOPE_DIM), ukv[..., :QK_NOPE_DIM]], -1)
    w_k = k_cat.reshape(depth, kv_rank, N_HEADS * HEAD_PAD).astype(BF16)
    w_vt = jnp.swapaxes(ukv[..., QK_NOPE_DIM:].reshape(depth, kv_rank, N_HEADS * V_HEAD_DIM), 1, 2).astype(BF16)
    row = lambda a: a.reshape(depth, 1, a.shape[-1])
    return dict(w_in=w_in_l, g_mix=row(g_mix), g_q=row(g_q), w_q=w_q, g_kv=row(g_kv), w_k=w_k, w_vt=w_vt,
                w_o=w_o_attn.astype(BF16), w_dw=w_dw, b_dw=row(b_dw), g_cn=row(g_cn), b_cn=row(b_cn),
                w_pw2=w_pw2.astype(BF16), w_out=w_out.astype(BF16), g_ffn=row(g_ffn),
                w_gu=w_gu.astype(BF16), w_down=w_down.astype(BF16))


def kernel(x, c, positions, w_ada, b_ada, g_mix, w_in, g_q, w_uq, g_kv, w_ukv, w_o_attn, w_dw, b_dw, g_cn, b_cn,
           w_pw2, w_out, g_ffn, w_gu, w_down, g_final):
    bsz, seq, d_model = x.shape
    depth = w_in.shape[0]
    tm = min(seq, 512)
    tq = min(seq, 512)
    p = _layout_params(w_in, g_mix, g_q, w_uq, g_kv, w_ukv, w_o_attn, w_dw, b_dw, g_cn, b_cn, w_pw2, w_out, g_ffn,
                       w_gu, w_down)
    ct, st = _rope_tables(positions)
    mod = _modulation(c, w_ada, b_ada).reshape(depth, bsz, N_MOD, 1, d_model)
    g_fin = g_final.reshape(1, d_model)
    for layer in range(depth):
        q, k, v, ga, gby = _pre_attention(layer, x, mod, ct, st, p, tm=tm)
        attn = _attention(q, k, v, tq=tq)
        x = _post_attention(layer, x, attn, ga, gby, mod, p, g_fin, tm=tm, final=layer == depth - 1)
    return x
```

```python
import functools
import math

import jax
import jax.numpy as jnp
from jax import lax
from jax.experimental import pallas as pl
from jax.experimental.pallas import tpu as pltpu

N_HEADS = 8
QK_NOPE_DIM = 64
QK_ROPE_DIM = 32
QK_DIM = QK_NOPE_DIM + QK_ROPE_DIM
V_HEAD_DIM = 64
ROPE_THETA = 10000.0
CONV_WIDTH = 31
N_MOD = 6
EPS = 1e-6
NEG_INF = -1e30

LANES = 128
HEAD_PAD = LANES
HALF_ROPE = QK_ROPE_DIM // 2
ONES_ROWS = 16
CONV_HALO = 32
CONV_ROW_BLOCK = 64
SUBLANES = 8
FFN_CHUNK = 256
VMEM_LIMIT_BYTES = 56 * 1024 * 1024

BF16 = jnp.bfloat16
F32 = jnp.float32


def _dot(a, b):
    return jnp.dot(a, b, preferred_element_type=F32)


def _rms(x, gain):
    return x * lax.rsqrt(jnp.mean(x * x, axis=-1, keepdims=True) + EPS) * gain


def _swap_halves(slab):
    return pltpu.roll(slab, HALF_ROPE, axis=1) + pltpu.roll(slab, HEAD_PAD - HALF_ROPE, axis=1)


def _const_spec(shape, index_map):
    return pl.BlockSpec(shape, index_map, pipeline_mode=pl.Buffered(1))


def _rope_table_kernel(pos_ref, invf_ref, sign_ref, ct_ref, st_ref):
    ang = pos_ref[...].astype(F32) * invf_ref[...]
    ct_ref[...] = jnp.cos(ang)
    st_ref[...] = jnp.sin(ang) * sign_ref[...]


def _rope_tables(positions):
    n_tok = positions.size
    tt = min(n_tok, 2048)
    inv_freq = ROPE_THETA ** (-jnp.arange(0, QK_ROPE_DIM, 2, dtype=F32) / QK_ROPE_DIM)
    zeros = lambda n: jnp.zeros((n,), F32)
    guard, tail = zeros(HALF_ROPE), zeros(HALF_ROPE + QK_NOPE_DIM)
    invf = jnp.concatenate([guard, inv_freq, inv_freq, tail])[None, :]
    sign = jnp.concatenate([guard, -jnp.ones((HALF_ROPE,), F32), jnp.ones((HALF_ROPE,), F32), tail])[None, :]
    pos_b = jnp.broadcast_to(positions.reshape(n_tok, 1), (n_tok, LANES))
    row = pl.BlockSpec((1, LANES), lambda i: (0, 0))
    tile = pl.BlockSpec((tt, LANES), lambda i: (i, 0))
    return pl.pallas_call(
        _rope_table_kernel,
        grid=(n_tok // tt,),
        in_specs=[tile, row, row],
        out_specs=[tile, tile],
        out_shape=[jax.ShapeDtypeStruct((n_tok, LANES), F32)] * 2,
        name="rope_tables",
    )(pos_b, invf, sign)


def _mod_kernel(c_ref, w_ref, b_ref, o_ref):
    c = c_ref[...]
    c_act = (c * jax.nn.sigmoid(c)).astype(BF16)
    o_ref[0] = _dot(c_act, w_ref[0].astype(BF16)) + b_ref[0]


def _modulation(c, w_ada, b_ada):
    depth, d_model, n_out = w_ada.shape
    bsz = c.shape[0]
    tn = d_model
    return pl.pallas_call(
        _mod_kernel,
        grid=(depth, n_out // tn),
        in_specs=[pl.BlockSpec((bsz, d_model), lambda l, j: (0, 0)),
                  pl.BlockSpec((1, d_model, tn), lambda l, j: (l, 0, j)),
                  pl.BlockSpec((1, 1, tn), lambda l, j: (l, 0, j))],
        out_specs=pl.BlockSpec((1, bsz, tn), lambda l, j: (l, 0, j)),
        out_shape=jax.ShapeDtypeStruct((depth, bsz, n_out), F32),
        name="adaln_modulation",
    )(c, w_ada, b_ada.reshape(depth, 1, n_out))


def _pre_attn_kernel(x_ref, mod_ref, gmix_ref, ct_ref, st_ref, win_ref, gq_ref, wq_ref, gkv_ref, wk_ref, wvt_ref,
                     wdw_ref, bdw_ref, gcn_ref, bcn_ref, wpw2_ref,
                     q_out, k_out, vt_out, ga_out, gby_out, uext_ref, conv_ref, *, tm, dims):
    q_rank, kv_rank, conv_ch, d_model = dims

    @pl.when(pl.program_id(1) == 0)
    def _():
        uext_ref[0:CONV_HALO, :] = jnp.zeros((CONV_HALO, conv_ch), F32)

    x = x_ref[0]
    sh1 = mod_ref[0, 0, 0]
    sc1 = mod_ref[0, 0, 1]
    hb = (_rms(x, gmix_ref[0] * (1.0 + sc1)) + sh1).astype(BF16)

    widths = dict(q_lat=q_rank, kv_lat=kv_rank, kr=HEAD_PAD, glu_a=conv_ch, glu_b=conv_ch,
                  gate_a=d_model, gate_b=d_model)
    starts, c0 = {}, 0
    for name, width in widths.items():
        starts[name] = c0
        c0 += width

    def proj(name):
        return _dot(hb, win_ref[0, :, starts[name]:starts[name] + widths[name]])

    uext_ref[CONV_HALO:CONV_HALO + tm, :] = proj("glu_a") * jax.nn.sigmoid(proj("glu_b"))
    first = CONV_HALO - (CONV_WIDTH - 1)
    win_rows = CONV_ROW_BLOCK + CONV_HALO
    for c in range(conv_ch // LANES):
        cl = slice(c * LANES, (c + 1) * LANES)
        for rb in range(tm // CONV_ROW_BLOCK):
            r0 = rb * CONV_ROW_BLOCK
            window = uext_ref[r0:r0 + win_rows, cl]
            acc = jnp.broadcast_to(bdw_ref[0, :, cl], (CONV_ROW_BLOCK, LANES))
            for res in range(SUBLANES):
                shifted = window if res == 0 else pltpu.roll(window, win_rows - res, axis=0)
                for off in range(res, CONV_HALO + 1, SUBLANES):
                    k = off - first
                    if 0 <= k < CONV_WIDTH:
                        acc = acc + wdw_ref[0, k:k + 1, cl] * shifted[off - res:off - res + CONV_ROW_BLOCK]
            conv_ref[r0:r0 + CONV_ROW_BLOCK, cl] = acc
    uext_ref[0:CONV_HALO, :] = uext_ref[tm:tm + CONV_HALO, :]

    ga_out[0] = jax.nn.sigmoid(proj("gate_a")).astype(BF16)

    ct = ct_ref[...]
    st = st_ref[...]
    q_scale = QK_DIM ** -0.5 * math.log2(math.e)
    ctq = ct * q_scale
    stq = st * q_scale

    qn = _rms(proj("q_lat"), gq_ref[0]).astype(BF16)
    q2 = _dot(qn, wq_ref[0])
    for h in range(N_HEADS):
        lo, hi = h * HEAD_PAD, (h + 1) * HEAD_PAD
        q_out[0, :, lo:hi] = (q2[:, lo:hi] * ctq + _swap_halves(q2[:, lo:hi]) * stq).astype(BF16)

    kvn = _rms(proj("kv_lat"), gkv_ref[0]).astype(BF16)
    kr = proj("kr")
    kr = kr * ct + _swap_halves(kr) * st
    k2 = _dot(kvn, wk_ref[0])
    for h in range(N_HEADS):
        lo, hi = h * HEAD_PAD, (h + 1) * HEAD_PAD
        k_out[0, :, lo:hi] = (k2[:, lo:hi] + kr).astype(BF16)
    vt_out[0] = lax.dot_general(wvt_ref[0], kvn, (((1,), (1,)), ((), ())), preferred_element_type=F32).astype(BF16)

    gate_b = jax.nn.sigmoid(proj("gate_b"))
    u = conv_ref[...]
    mu = jnp.mean(u, axis=-1, keepdims=True)
    uc = u - mu
    un = uc * lax.rsqrt(jnp.mean(uc * uc, axis=-1, keepdims=True) + EPS) * gcn_ref[0] + bcn_ref[0]
    y_conv = _dot((un * jax.nn.sigmoid(un)).astype(BF16), wpw2_ref[0])
    gby_out[0] = (gate_b * y_conv).astype(BF16)


def _pre_attention(layer, x, mod, ct, st, p, *, tm):
    bsz, seq, d_model = x.shape
    q_rank, kv_rank, conv_ch = p["g_q"].shape[-1], p["g_kv"].shape[-1], p["g_cn"].shape[-1]
    n_qk = N_HEADS * HEAD_PAD
    n_v = N_HEADS * V_HEAD_DIM
    n_s = seq // tm
    l = layer
    tok = lambda w: pl.BlockSpec((1, tm, w), lambda b, s: (b, s, 0))
    wspec = lambda a: _const_spec((1,) + a.shape[1:], lambda b, s: (l,) + (0,) * (a.ndim - 1))
    tab = pl.BlockSpec((tm, LANES), lambda b, s: (b * n_s + s, 0))
    weights = [p["w_in"], p["g_q"], p["w_q"], p["g_kv"], p["w_k"], p["w_vt"], p["w_dw"], p["b_dw"], p["g_cn"],
               p["b_cn"], p["w_pw2"]]
    kern = functools.partial(_pre_attn_kernel, tm=tm, dims=(q_rank, kv_rank, conv_ch, d_model))
    out_bf16 = lambda w: jax.ShapeDtypeStruct((bsz, seq, w), BF16)
    return pl.pallas_call(
        kern,
        grid=(bsz, n_s),
        in_specs=[tok(d_model),
                  pl.BlockSpec((1, 1, N_MOD, 1, d_model), lambda b, s: (l, b, 0, 0, 0)),
                  wspec(p["g_mix"]), tab, tab] + [wspec(w) for w in weights],
        out_specs=[tok(n_qk), tok(n_qk), pl.BlockSpec((1, n_v, tm), lambda b, s: (b, 0, s)), tok(d_model),
                   tok(d_model)],
        out_shape=[out_bf16(n_qk), out_bf16(n_qk), jax.ShapeDtypeStruct((bsz, n_v, seq), BF16), out_bf16(d_model),
                   out_bf16(d_model)],
        scratch_shapes=[pltpu.VMEM((CONV_HALO + tm, conv_ch), F32), pltpu.VMEM((tm, conv_ch), F32)],
        compiler_params=pltpu.CompilerParams(dimension_semantics=("arbitrary", "arbitrary"),
                                             vmem_limit_bytes=VMEM_LIMIT_BYTES),
        name="pre_attention",
    )(x, mod, p["g_mix"], ct, st, *weights)


def _attn_kernel(q_ref, k_ref, vt_ref, o_ref, m_ref, acc_ref, sa_ref, sb_ref, *, tq):
    heads = q_ref.shape[-1] // HEAD_PAD
    n_blocks = q_ref.shape[1] // tq
    half = tq // 2
    bufs = (sa_ref, sb_ref)
    steps = []
    for qi in range(n_blocks):
        for j in range(qi):
            steps.append([(qi, 0, tq, j * tq, (j + 1) * tq)])
        steps.append([(qi, 0, half, qi * tq, qi * tq + half), (qi, half, tq, qi * tq, (qi + 1) * tq)])

    def scores(step, s_ref):
        for qi, c0, c1, k0, k1 in step:
            for h in range(heads):
                hl = slice(h * HEAD_PAD, (h + 1) * HEAD_PAD)
                s_ref[h, 0:k1 - k0, c0:c1] = lax.dot_general(
                    k_ref[0, k0:k1, hl], q_ref[0, qi * tq + c0:qi * tq + c1, hl], (((1,), (1,)), ((), ())),
                    preferred_element_type=F32)

    def update(step, s_ref):
        for qi, c0, c1, k0, k1 in step:
            diagonal = k1 > qi * tq
            for h in range(heads):
                vl = slice(h * V_HEAD_DIM, (h + 1) * V_HEAD_DIM)
                s_t = s_ref[h, 0:k1 - k0, c0:c1]
                if diagonal:
                    r0 = qi * tq + c0 - k0
                    tri = s_t[r0:]
                    key_id = lax.broadcasted_iota(jnp.int32, tri.shape, 0)
                    qry_id = lax.broadcasted_iota(jnp.int32, tri.shape, 1)
                    tri = jnp.where(key_id <= qry_id, tri, NEG_INF)
                    s_t = tri if r0 == 0 else jnp.concatenate([s_t[:r0], tri], axis=0)
                m_new = jnp.max(s_t, axis=0, keepdims=True)
                if k0 > 0:
                    m_prev = m_ref[h, :, c0:c1]
                    m_new = jnp.maximum(m_prev, m_new)
                p_t = jnp.exp2(s_t - m_new).astype(BF16)
                v_ext = jnp.concatenate([vt_ref[0, vl, k0:k1], jnp.ones((ONES_ROWS, k1 - k0), BF16)], axis=0)
                acc = _dot(v_ext, p_t)
                if k0 > 0:
                    acc = jnp.exp2(m_prev - m_new) * acc_ref[h, :, c0:c1] + acc
                if not diagonal:
                    acc_ref[h, :, c0:c1] = acc
                    m_ref[h, :, c0:c1] = m_new
                else:
                    o_ref[0, vl, qi * tq + c0:qi * tq + c1] = (
                        acc[:V_HEAD_DIM] / acc[V_HEAD_DIM:V_HEAD_DIM + 1]).astype(BF16)

    scores(steps[0], bufs[0])
    for n, step in enumerate(steps):
        if n + 1 < len(steps):
            scores(steps[n + 1], bufs[(n + 1) % 2])
        update(step, bufs[n % 2])


def _attention(q, k, vt, *, tq):
    bsz, seq, _ = q.shape
    heads_per_step = LANES // V_HEAD_DIM
    n_hp = N_HEADS // heads_per_step
    qk_w = heads_per_step * HEAD_PAD
    kern = functools.partial(_attn_kernel, tq=tq)
    return pl.pallas_call(
        kern,
        grid=(bsz, n_hp),
        in_specs=[pl.BlockSpec((1, seq, qk_w), lambda b, h: (b, 0, h)),
                  pl.BlockSpec((1, seq, qk_w), lambda b, h: (b, 0, h)),
                  pl.BlockSpec((1, LANES, seq), lambda b, h: (b, h, 0))],
        out_specs=pl.BlockSpec((1, LANES, seq), lambda b, h: (b, h, 0)),
        out_shape=jax.ShapeDtypeStruct((bsz, N_HEADS * V_HEAD_DIM, seq), BF16),
        scratch_shapes=[pltpu.VMEM((heads_per_step, 1, tq), F32),
                        pltpu.VMEM((heads_per_step, V_HEAD_DIM + ONES_ROWS, tq), F32),
                        pltpu.VMEM((heads_per_step, tq, tq), F32), pltpu.VMEM((heads_per_step, tq, tq), F32)],
        compiler_params=pltpu.CompilerParams(dimension_semantics=("arbitrary", "arbitrary"),
                                             vmem_limit_bytes=VMEM_LIMIT_BYTES),
        name="mla_attention",
    )(q, k, vt)


def _post_attn_kernel(x_ref, attn_ref, ga_ref, gby_ref, mod_ref, wo_ref, wout_ref, gffn_ref, wgu_ref, wdown_ref,
                      gfin_ref, o_ref, *, ffn_chunk, final):
    x = x_ref[0]
    gt1 = mod_ref[0, 0, 2]
    sh2 = mod_ref[0, 0, 3]
    sc2 = mod_ref[0, 0, 4]
    gt2 = mod_ref[0, 0, 5]
    y_attn = lax.dot_general(attn_ref[0], wo_ref[0], (((0,), (0,)), ((), ())), preferred_element_type=F32)
    y = ga_ref[0].astype(F32) * y_attn + gby_ref[0].astype(F32)
    x1 = x + gt1 * _dot(y.astype(BF16), wout_ref[0])

    hb = (_rms(x1, gffn_ref[0] * (1.0 + sc2)) + sh2).astype(BF16)
    hidden = wdown_ref.shape[1]
    ffn = None
    for c0 in range(0, hidden, ffn_chunk):
        c1 = min(c0 + ffn_chunk, hidden)
        g = _dot(hb, wgu_ref[0, :, c0:c1])
        up = _dot(hb, wgu_ref[0, :, hidden + c0:hidden + c1])
        part = _dot((g * jax.nn.sigmoid(g) * up).astype(BF16), wdown_ref[0, c0:c1, :])
        ffn = part if ffn is None else ffn + part
    x2 = x1 + gt2 * ffn
    if final:
        x2 = _rms(x2, gfin_ref[...])
    o_ref[0] = x2


def _post_attention(layer, x, attn, ga, gby, mod, p, g_final, *, tm, final):
    bsz, seq, d_model = x.shape
    l = layer
    tok = lambda w: pl.BlockSpec((1, tm, w), lambda b, s: (b, s, 0))
    wspec = lambda a: _const_spec((1,) + a.shape[1:], lambda b, s: (l,) + (0,) * (a.ndim - 1))
    weights = [p["w_o"], p["w_out"], p["g_ffn"], p["w_gu"], p["w_down"]]
    kern = functools.partial(_post_attn_kernel, ffn_chunk=FFN_CHUNK, final=final)
    return pl.pallas_call(
        kern,
        grid=(bsz, seq // tm),
        in_specs=[tok(d_model), pl.BlockSpec((1, attn.shape[1], tm), lambda b, s: (b, 0, s)), tok(d_model),
                  tok(d_model),
                  pl.BlockSpec((1, 1, N_MOD, 1, d_model), lambda b, s: (l, b, 0, 0, 0))]
                 + [wspec(w) for w in weights]
                 + [_const_spec((1, d_model), lambda b, s: (0, 0))],
        out_specs=tok(d_model),
        out_shape=jax.ShapeDtypeStruct((bsz, seq, d_model), F32),
        compiler_params=pltpu.CompilerParams(dimension_semantics=("arbitrary", "arbitrary"),
                                             vmem_limit_bytes=VMEM_LIMIT_BYTES),
        name="post_attention",
    )(x, attn, ga, gby, mod, *weights, g_final)


def _layout_params(w_in, g_mix, g_q, w_uq, g_kv, w_ukv, w_o_attn, w_dw, b_dw, g_cn, b_cn, w_pw2, w_out, g_ffn,
                   w_gu, w_down):
    depth, d_model, _ = w_in.shape
    q_rank, kv_rank, conv_ch = g_q.shape[-1], g_kv.shape[-1], g_cn.shape[-1]
    zeros = lambda *s: jnp.zeros((depth,) + s, F32)
    o = 0
    def take(width):
        nonlocal o
        piece = w_in[:, :, o:o + width]
        o += width
        return piece
    q_lat, kv_lat, k_rope = take(q_rank), take(kv_rank), take(QK_ROPE_DIM)
    glu, gate_a, gate_b = take(2 * conv_ch), take(d_model), take(d_model)
    kr1, kr2 = k_rope[..., :HALF_ROPE], k_rope[..., HALF_ROPE:]
    kr_pad = jnp.concatenate([zeros(d_model, HALF_ROPE), kr1, kr2, zeros(d_model, HALF_ROPE + QK_NOPE_DIM)], -1)
    w_in_l = jnp.concatenate([q_lat, kv_lat, kr_pad, glu, gate_a, gate_b], -1).astype(BF16)

    uq = w_uq.reshape(depth, q_rank, N_HEADS, QK_DIM)
    nope, r1, r2 = uq[..., :QK_NOPE_DIM], uq[..., QK_NOPE_DIM:QK_NOPE_DIM + HALF_ROPE], uq[..., QK_NOPE_DIM + HALF_ROPE:]
    guard = zeros(q_rank, N_HEADS, HALF_ROPE)
    w_q = jnp.concatenate([guard, r1, r2, guard, nope], -1).reshape(depth, q_rank, N_HEADS * HEAD_PAD).astype(BF16)

    ukv = w_ukv.reshape(depth, kv_rank, N_HEADS, QK_NOPE_DIM + V_HEAD_DIM)
    k_cat = jnp.concatenate([zeros(kv_rank, N_HEADS, HEAD_PAD - QK_NOPE_DIM), ukv[..., :QK_NOPE_DIM]], -1)
    w_k = k_cat.reshape(depth, kv_rank, N_HEADS * HEAD_PAD).astype(BF16)
    w_vt = jnp.swapaxes(ukv[..., QK_NOPE_DIM:].reshape(depth, kv_rank, N_HEADS * V_HEAD_DIM), 1, 2).astype(BF16)
    row = lambda a: a.reshape(depth, 1, a.shape[-1])
    return dict(w_in=w_in_l, g_mix=row(g_mix), g_q=row(g_q), w_q=w_q, g_kv=row(g_kv), w_k=w_k, w_vt=w_vt,
                w_o=w_o_attn.astype(BF16), w_dw=w_dw, b_dw=row(b_dw), g_cn=row(g_cn), b_cn=row(b_cn),
                w_pw2=w_pw2.astype(BF16), w_out=w_out.astype(BF16), g_ffn=row(g_ffn),
                w_gu=w_gu.astype(BF16), w_down=w_down.astype(BF16))


def kernel(x, c, positions, w_ada, b_ada, g_mix, w_in, g_q, w_uq, g_kv, w_ukv, w_o_attn, w_dw, b_dw, g_cn, b_cn,
           w_pw2, w_out, g_ffn, w_gu, w_down, g_final):
    bsz, seq, d_model = x.shape
    depth = w_in.shape[0]
    tm = min(seq, 512)
    tq = min(seq, 512)
    p = _layout_params(w_in, g_mix, g_q, w_uq, g_kv, w_ukv, w_o_attn, w_dw, b_dw, g_cn, b_cn, w_pw2, w_out, g_ffn,
                       w_gu, w_down)
    ct, st = _rope_tables(positions)
    mod = _modulation(c, w_ada, b_ada).reshape(depth, bsz, N_MOD, 1, d_model)
    g_fin = g_final.reshape(1, d_model)
    for layer in range(depth):
        q, k, v, ga, gby = _pre_attention(layer, x, mod, ct, st, p, tm=tm)
        attn = _attention(q, k, v, tq=tq)
        x = _post_attention(layer, x, attn, ga, gby, mod, p, g_fin, tm=tm, final=layer == depth - 1)
    return x
```
